```python
import math
import jax, jax.numpy as jnp
from jax import lax
import numpy as np

D_MODEL = 1024
BATCH = 8
SEQ = 8192
DEPTH = 1

HEAD_DIM = 64
POOL_WINDOWS = (2, 4, 8, 16)
POOL_GROUP_DIM = D_MODEL // 16
POOL_DIM = len(POOL_WINDOWS) * POOL_GROUP_DIM
ATTN_GROUPS = ((128, 1), (512, 4), (2048, 16))
N_ATTN_HEADS = (D_MODEL - POOL_DIM) // HEAD_DIM
HEADS_PER_GROUP = N_ATTN_HEADS // len(ATTN_GROUPS)
ATTN_DIM = N_ATTN_HEADS * HEAD_DIM
D_MIX = POOL_DIM + ATTN_DIM
D_IN = POOL_DIM + 3 * ATTN_DIM
D_FF = 64 * int(math.ceil(8 * D_MODEL / (3 * 64)))
RMS_EPS = 1e-6

kernel_name = "hymba_pool_dilated_alibi_macaron"


def alibi_slopes():
    s = np.array([2.0 ** (-8.0 * (i + 1) / N_ATTN_HEADS) for i in range(N_ATTN_HEADS)], np.float32)
    return jnp.asarray(s)


def rmsnorm(x, g):
    xf = x.astype(jnp.float32)
    y = xf * lax.rsqrt(jnp.mean(xf * xf, axis=-1, keepdims=True) + RMS_EPS)
    return (y * g.astype(jnp.float32)).astype(x.dtype)


def swiglu(h, w_gate, w_up, w_down):
    return (jax.nn.silu(h @ w_gate) * (h @ w_up)) @ w_down


def pool_mixer(u, w_lin, scale):
    B, S, _ = u.shape
    G = len(POOL_WINDOWS)
    uf = u.astype(jnp.float32).reshape(B, S, G, POOL_GROUP_DIM)
    csum = jnp.concatenate([jnp.zeros((B, 1, G, POOL_GROUP_DIM), jnp.float32),
                            jnp.cumsum(uf, axis=1)], axis=1)
    t = jnp.arange(S)
    outs = []
    for g, w in enumerate(POOL_WINDOWS):
        lo = jnp.clip(t - w // 2, 0, S)
        hi = jnp.clip(t - w // 2 + w, 0, S)
        cg = csum[:, :, g]
        win_sum = jnp.take(cg, hi, axis=1) - jnp.take(cg, lo, axis=1)
        mean = win_sum / (hi - lo).astype(jnp.float32)[None, :, None]
        outs.append(mean - uf[:, :, g])
    y = jnp.stack(outs, axis=2).astype(u.dtype)
    y = jnp.einsum('bsgc,gce->bsge', y, w_lin)
    return y.reshape(B, S, POOL_DIM) * scale


def dilated_window_attention(q, k, v, window, dilation, slopes):
    B, S, H, Dh = q.shape
    n_side = (window // 2) // dilation
    blk = n_side
    L = S // dilation
    nb = -(-L // blk)
    Lp = nb * blk

    def to_sub(a):
        return a.reshape(B, L, dilation, H, Dh).transpose(0, 2, 1, 3, 4).reshape(B * dilation, L, H, Dh)

    def windows(a):
        a = jnp.pad(to_sub(a), ((0, 0), (blk, Lp - L + blk), (0, 0), (0, 0)))
        a = a.reshape(-1, nb + 2, blk, H, Dh)
        return jnp.concatenate([a[:, :-2], a[:, 1:-1], a[:, 2:]], axis=2)

    qs = jnp.pad(to_sub(q), ((0, 0), (0, Lp - L), (0, 0), (0, 0))).reshape(-1, nb, blk, H, Dh)
    kw, vw = windows(k), windows(v)

    rel = jnp.arange(3 * blk)[None, :] - blk - jnp.arange(blk)[:, None]
    key_idx = (jnp.arange(nb)[:, None] - 1) * blk + jnp.arange(3 * blk)[None, :]
    valid = (jnp.abs(rel) <= n_side)[None] & ((key_idx >= 0) & (key_idx < L))[:, None, :]
    dist = (jnp.abs(rel) * dilation).astype(jnp.float32)
    bias = -slopes.astype(jnp.float32)[:, None, None] * dist[None]

    s = jnp.einsum('znqhd,znkhd->znhqk', qs, kw).astype(jnp.float32) * (Dh ** -0.5)
    s = jnp.where(valid[None, :, None], s + bias[None, None], -jnp.inf)
    m = jnp.max(s, axis=-1, keepdims=True)
    p = jnp.exp(s - m)
    den = jnp.sum(p, axis=-1)
    o = jnp.einsum('znhqk,znkhd->znqhd', p, vw.astype(jnp.float32))
    den_t = jnp.swapaxes(den, 2, 3)
    o = o / den_t[..., None]
    lse = jnp.swapaxes(m[..., 0], 2, 3) + jnp.log(den_t)

    def from_sub(a):
        a = a.reshape((B, dilation, Lp) + a.shape[3:])[:, :, :L]
        a = jnp.swapaxes(a, 1, 2)
        return a.reshape((B, S) + a.shape[3:])

    return from_sub(o), from_sub(lse)


def attention_mixer(q, k, v):
    B, S = q.shape[:2]
    slopes = alibi_slopes()
    outs, lses = [], []
    for gi, (window, dilation) in enumerate(ATTN_GROUPS):
        hs = slice(gi * HEADS_PER_GROUP, (gi + 1) * HEADS_PER_GROUP)
        o, lse = dilated_window_attention(q[:, :, hs], k[:, :, hs], v[:, :, hs],
                                          window, dilation, slopes[hs])
        outs.append(o)
        lses.append(lse)
    alpha = jax.nn.softmax(jnp.stack(lses, axis=0), axis=0)
    o = jnp.concatenate([outs[g] * alpha[g][..., None] for g in range(len(ATTN_GROUPS))], axis=2)
    return o.reshape(B, S, ATTN_DIM).astype(q.dtype)


def setup_inputs(seed: int = 0) -> dict:
    key = jax.random.key(seed)
    ks = jax.random.split(key, 20)
    f32 = jnp.float32

    def nrm(k, shape, fan_in):
        return jax.random.normal(k, shape, f32) * (fan_in ** -0.5)

    def gain(k, n):
        return 1.0 + 0.02 * jax.random.normal(k, (DEPTH, n), f32)

    return {
        "x": jax.random.normal(ks[0], (BATCH, SEQ, D_MODEL), f32),
        "g_ffn1_pre": gain(ks[1], D_MODEL),
        "w1_gate": nrm(ks[2], (DEPTH, D_MODEL, D_FF), D_MODEL),
        "w1_up": nrm(ks[3], (DEPTH, D_MODEL, D_FF), D_MODEL),
        "w1_down": nrm(ks[4], (DEPTH, D_FF, D_MODEL), D_FF),
        "g_ffn1_post": gain(ks[5], D_MODEL),
        "g_mix_pre": gain(ks[6], D_MODEL),
        "w_in": nrm(ks[7], (DEPTH, D_MODEL, D_IN), D_MODEL),
        "w_pool_lin": nrm(ks[8], (DEPTH, len(POOL_WINDOWS), POOL_GROUP_DIM, POOL_GROUP_DIM), POOL_GROUP_DIM),
        "pool_scale": gain(ks[9], POOL_DIM),
        "w_out": nrm(ks[10], (DEPTH, D_MIX, D_MODEL), D_MIX),
        "g_mix_post": gain(ks[11], D_MODEL),
        "g_ffn2_pre": gain(ks[12], D_MODEL),
        "w2_gate": nrm(ks[13], (DEPTH, D_MODEL, D_FF), D_MODEL),
        "w2_up": nrm(ks[14], (DEPTH, D_MODEL, D_FF), D_MODEL),
        "w2_down": nrm(ks[15], (DEPTH, D_FF, D_MODEL), D_FF),
        "g_ffn2_post": gain(ks[16], D_MODEL),
    }


def reference(x, g_ffn1_pre, w1_gate, w1_up, w1_down, g_ffn1_post, g_mix_pre, w_in,
              w_pool_lin, pool_scale, w_out, g_mix_post, g_ffn2_pre, w2_gate, w2_up,
              w2_down, g_ffn2_post):
    B, S, _ = x.shape
    for l in range(DEPTH):
        x = x + 0.5 * rmsnorm(swiglu(rmsnorm(x, g_ffn1_pre[l]), w1_gate[l], w1_up[l], w1_down[l]),
                              g_ffn1_post[l])
        h = rmsnorm(x, g_mix_pre[l])
        z = h @ w_in[l]
        u = z[..., :POOL_DIM]
        qkv = z[..., POOL_DIM:].reshape(B, S, 3, N_ATTN_HEADS, HEAD_DIM)
        a_pool = pool_mixer(u, w_pool_lin[l], pool_scale[l])
        a_attn = attention_mixer(qkv[:, :, 0], qkv[:, :, 1], qkv[:, :, 2])
        mix = jnp.concatenate([a_pool.astype(x.dtype), a_attn.astype(x.dtype)], axis=-1) @ w_out[l]
        x = x + rmsnorm(mix, g_mix_post[l])
        x = x + 0.5 * rmsnorm(swiglu(rmsnorm(x, g_ffn2_pre[l]), w2_gate[l], w2_up[l], w2_down[l]),
                              g_ffn2_post[l])
    return x
```

```python
import functools

import numpy as np
import jax
import jax.numpy as jnp
from jax import lax
from jax.experimental import pallas as pl
from jax.experimental.pallas import tpu as pltpu

D_MODEL = 1024
HEAD_DIM = 64
POOL_WINDOWS = (2, 4, 8, 16)
POOL_GROUP_DIM = 64
POOL_DIM = len(POOL_WINDOWS) * POOL_GROUP_DIM
ATTN_GROUPS = ((128, 1), (512, 4), (2048, 16))
HEADS_PER_GROUP = 4
N_ATTN_HEADS = HEADS_PER_GROUP * len(ATTN_GROUPS)
GROUP_DIM = HEADS_PER_GROUP * HEAD_DIM
ATTN_DIM = N_ATTN_HEADS * HEAD_DIM
D_QKV = 3 * ATTN_DIM
D_FF = 2752
RMS_EPS = 1e-6

V7X_MXU_DIM = 256
V7X_VMEM_LIMIT_BYTES = 56 * 1024 * 1024
FF_PAD = -(-D_FF // V7X_MXU_DIM) * V7X_MXU_DIM
N_SIDE = 64
Q_SUB = 128
K_WIN = Q_SUB + 2 * N_SIDE
POOL_HALO = 8
MASKED = -1e30

F32 = jnp.float32
BF16 = jnp.bfloat16


def _rmsnorm(x, g):
    ms = jnp.mean(x * x, axis=-1, keepdims=True)
    return x * lax.rsqrt(ms + RMS_EPS) * g


def _const_spec(shape):
    return pl.BlockSpec(shape, lambda *_: (0,) * len(shape), pipeline_mode=pl.Buffered(1))


def _params(n_axes):
    return pltpu.CompilerParams(dimension_semantics=("arbitrary",) * n_axes,
                                vmem_limit_bytes=V7X_VMEM_LIMIT_BYTES)


def _ffn_kernel(x_ref, gpre_ref, wg_ref, wu_ref, wd_ref, gpost_ref, o_ref, acc_ref, *, ff_chunk):
    x = x_ref[...]
    xn = _rmsnorm(x, gpre_ref[...]).astype(BF16)
    for c0 in range(0, FF_PAD, ff_chunk):
        gate = jnp.dot(xn, wg_ref[:, c0:c0 + ff_chunk], preferred_element_type=F32)
        up = jnp.dot(xn, wu_ref[:, c0:c0 + ff_chunk], preferred_element_type=F32)
        act = (gate * jax.nn.sigmoid(gate) * up).astype(BF16)
        part = jnp.dot(act, wd_ref[c0:c0 + ff_chunk, :], preferred_element_type=F32)
        if c0 == 0:
            acc_ref[...] = part
        else:
            acc_ref[...] += part
    o_ref[...] = x + 0.5 * _rmsnorm(acc_ref[...], gpost_ref[...])


def _ffn(x, g_pre, wg, wu, wd, g_post, *, tm, ff_chunk):
    n = x.shape[0]
    row = pl.BlockSpec((tm, D_MODEL), lambda i: (i, 0))
    return pl.pallas_call(
        functools.partial(_ffn_kernel, ff_chunk=ff_chunk),
        grid=(n // tm,),
        in_specs=[row, _const_spec((1, D_MODEL)), _const_spec((D_MODEL, FF_PAD)),
                  _const_spec((D_MODEL, FF_PAD)), _const_spec((FF_PAD, D_MODEL)),
                  _const_spec((1, D_MODEL))],
        out_specs=row,
        out_shape=jax.ShapeDtypeStruct((n, D_MODEL), F32),
        scratch_shapes=[pltpu.VMEM((tm, D_MODEL), F32)],
        compiler_params=_params(1),
        name="ffn",
    )(x, g_pre, wg, wu, wd, g_post)


def _proj_in_kernel(x_ref, g_ref, wu_ref, wqkv_ref, u_ref, qkv_ref):
    h = _rmsnorm(x_ref[...], g_ref[...]).astype(BF16)
    u_ref[...] = jnp.dot(h, wu_ref[...], preferred_element_type=F32)
    qkv_ref[...] = jnp.dot(h, wqkv_ref[...], preferred_element_type=F32).astype(BF16)


def _proj_in(x, g, w_pool_in, w_qkv, *, tm):
    n = x.shape[0]
    return pl.pallas_call(
        _proj_in_kernel,
        grid=(n // tm,),
        in_specs=[pl.BlockSpec((tm, D_MODEL), lambda i: (i, 0)), _const_spec((1, D_MODEL)),
                  _const_spec((D_MODEL, POOL_DIM)), _const_spec((D_MODEL, D_QKV))],
        out_specs=[pl.BlockSpec((tm, POOL_DIM), lambda i: (i, 0)),
                   pl.BlockSpec((tm, D_QKV), lambda i: (i, 0))],
        out_shape=[jax.ShapeDtypeStruct((n, POOL_DIM), F32),
                   jax.ShapeDtypeStruct((n, D_QKV), BF16)],
        compiler_params=_params(1),
        name="proj_in",
    )(x, g, w_pool_in, w_qkv)


def _attn_kernel(q_ref, kp_ref, kc_ref, kn_ref, vp_ref, vc_ref, vn_ref, o_ref, lse_ref,
                 kwin, vwin, bias_ref, *, dilation, slopes, sub_len, tq):
    first = (pl.program_id(0) == 0) & (pl.program_id(1) == 0) & (pl.program_id(2) == 0)

    @pl.when(first)
    def _():
        qi = lax.broadcasted_iota(jnp.int32, (Q_SUB, K_WIN), 0)
        kj = lax.broadcasted_iota(jnp.int32, (Q_SUB, K_WIN), 1)
        rel = jnp.abs(kj - N_SIDE - qi)
        dist = (rel * dilation).astype(F32)
        for h in range(HEADS_PER_GROUP):
            bias_ref[h] = jnp.where(rel <= N_SIDE, -slopes[h] * dist, MASKED)

    kwin[0:N_SIDE] = kp_ref[...]
    kwin[N_SIDE:N_SIDE + tq] = kc_ref[...]
    kwin[N_SIDE + tq:] = kn_ref[...]
    vwin[0:N_SIDE] = vp_ref[...]
    vwin[N_SIDE:N_SIDE + tq] = vc_ref[...]
    vwin[N_SIDE + tq:] = vn_ref[...]

    base = pl.program_id(2) * tq - N_SIDE
    lane = lax.broadcasted_iota(jnp.int32, (1, GROUP_DIM), 1)
    head_of_lane = lane >> 6
    key_col = lax.broadcasted_iota(jnp.int32, (1, K_WIN), 1)

    def sub_block(j, carry):
        r0 = pl.multiple_of(j * Q_SUB, Q_SUB)
        q = q_ref[pl.ds(r0, Q_SUB), :]
        kw = kwin[pl.ds(r0, K_WIN), :]
        vw = vwin[pl.ds(r0, K_WIN), :]
        kpos = base + r0 + key_col
        kvalid = (kpos >= 0) & (kpos < sub_len)
        o_sel = den_sel = lse_sel = None
        for h in range(HEADS_PER_GROUP):
            qh = jnp.where(head_of_lane == h, q, jnp.zeros_like(q))
            s = lax.dot_general(qh, kw, (((1,), (1,)), ((), ())), preferred_element_type=F32)
            s = jnp.where(kvalid, s * (HEAD_DIM ** -0.5) + bias_ref[h], MASKED)
            m = jnp.max(s, axis=-1, keepdims=True)
            p = jnp.exp(s - m)
            den = jnp.sum(p, axis=-1, keepdims=True)
            oh = jnp.dot(p.astype(BF16), vw, preferred_element_type=F32)
            lse = m + jnp.log(den)
            if h == 0:
                o_sel = oh
                den_sel = jnp.broadcast_to(den, oh.shape)
                lse_sel = jnp.broadcast_to(lse, oh.shape)
            else:
                mine = head_of_lane == h
                o_sel = jnp.where(mine, oh, o_sel)
                den_sel = jnp.where(mine, den, den_sel)
                lse_sel = jnp.where(mine, lse, lse_sel)
        o_ref[pl.ds(r0, Q_SUB), :] = o_sel / den_sel
        lse_ref[pl.ds(r0, Q_SUB), :] = lse_sel
        return carry

    lax.fori_loop(0, tq // Q_SUB, sub_block, 0)


def _attention_group(qkv, group, dilation, batch, seq):
    sub_len = seq // dilation
    tq = min(512, sub_len)
    assert sub_len % tq == 0 and tq % Q_SUB == 0
    halo_per_tq = tq // N_SIDE
    n_halo = sub_len // N_SIDE
    col_blocks = D_QKV // GROUP_DIM
    heads_total = ATTN_DIM // GROUP_DIM
    view = qkv.reshape(batch, sub_len, dilation * D_QKV)
    slopes = tuple(float(np.float32(2.0 ** (-8.0 * (group * HEADS_PER_GROUP + h + 1) / N_ATTN_HEADS)))
                   for h in range(HEADS_PER_GROUP))

    def col(r, part):
        return r * col_blocks + part * heads_total + group

    def cur(part):
        return pl.BlockSpec((None, tq, GROUP_DIM), lambda b, r, i: (b, i, col(r, part)))

    def prev(part):
        return pl.BlockSpec((None, N_SIDE, GROUP_DIM),
                            lambda b, r, i: (b, jnp.maximum(i * halo_per_tq - 1, 0), col(r, part)))

    def nxt(part):
        return pl.BlockSpec((None, N_SIDE, GROUP_DIM),
                            lambda b, r, i: (b, jnp.minimum((i + 1) * halo_per_tq, n_halo - 1), col(r, part)))

    out_spec = pl.BlockSpec((None, tq, GROUP_DIM), lambda b, r, i: (b, i, r))
    out_shape = jax.ShapeDtypeStruct((batch, sub_len, dilation * GROUP_DIM), F32)
    o, lse = pl.pallas_call(
        functools.partial(_attn_kernel, dilation=dilation, slopes=slopes, sub_len=sub_len, tq=tq),
        grid=(batch, dilation, sub_len // tq),
        in_specs=[cur(0), prev(1), cur(1), nxt(1), prev(2), cur(2), nxt(2)],
        out_specs=[out_spec, out_spec],
        out_shape=[out_shape, out_shape],
        scratch_shapes=[pltpu.VMEM((tq + 2 * N_SIDE, GROUP_DIM), BF16),
                        pltpu.VMEM((tq + 2 * N_SIDE, GROUP_DIM), BF16),
                        pltpu.VMEM((HEADS_PER_GROUP, Q_SUB, K_WIN), F32)],
        compiler_params=_params(3),
        name=f"attn_d{dilation}",
    )(view, view, view, view, view, view, view)
    n = batch * seq
    return o.reshape(n, GROUP_DIM), lse.reshape(n, GROUP_DIM)


def _pool_mean_minus_token(u_ref, up_ref, un_ref, ubuf, abuf, *, tm, seq):
    tiles_per_seq = seq // tm
    ti = pl.program_id(0) % tiles_per_seq
    n = tm + 2 * POOL_HALO
    zero_halo = jnp.zeros((POOL_HALO, POOL_DIM), F32)
    ubuf[0:POOL_HALO] = jnp.where(ti == 0, zero_halo, up_ref[...])
    ubuf[POOL_HALO:POOL_HALO + tm] = u_ref[...]
    ubuf[POOL_HALO + tm:n] = jnp.where(ti == tiles_per_seq - 1, zero_halo, un_ref[...])
    ubuf[n:] = zero_halo
    abuf[n:] = zero_halo
    a2 = ubuf[0:n] + ubuf[1:n + 1]
    abuf[0:n] = a2
    a4 = a2 + abuf[2:n + 2]
    ubuf[0:n] = a4
    a8 = a4 + ubuf[4:n + 4]
    abuf[0:n] = a8
    a16 = a8 + abuf[8:n + 8]
    ubuf[0:n] = a2
    s2 = ubuf[7:7 + tm]
    ubuf[0:n] = a4
    s4 = ubuf[6:6 + tm]
    s8 = abuf[4:4 + tm]
    s16 = a16[0:tm]
    lane = lax.broadcasted_iota(jnp.int32, (1, POOL_DIM), 1)
    grp = lane >> 6
    win_sum = jnp.where(grp == 0, s2, jnp.where(grp == 1, s4, jnp.where(grp == 2, s8, s16)))
    t = ti * tm + lax.broadcasted_iota(jnp.int32, (tm, 1), 0)
    half = jnp.where(grp == 0, 1, jnp.where(grp == 1, 2, jnp.where(grp == 2, 4, 8)))
    count = jnp.minimum(t + half, seq) - jnp.maximum(t - half, 0)
    return win_sum / count.astype(F32) - u_ref[...]


def _mix_out_kernel(x_ref, u_ref, up_ref, un_ref, o1_ref, o2_ref, o3_ref, l1_ref, l2_ref, l3_ref,
                    wpool_ref, pscale_ref, wout_ref, gpost_ref, out_ref, ubuf, abuf, *, tm, seq):
    y = _pool_mean_minus_token(u_ref, up_ref, un_ref, ubuf, abuf, tm=tm, seq=seq)
    a_pool = jnp.dot(y.astype(BF16), wpool_ref[...], preferred_element_type=F32) * pscale_ref[...]
    l1, l2, l3 = l1_ref[...], l2_ref[...], l3_ref[...]
    m = jnp.maximum(jnp.maximum(l1, l2), l3)
    e1, e2, e3 = jnp.exp(l1 - m), jnp.exp(l2 - m), jnp.exp(l3 - m)
    z = e1 + e2 + e3
    cat = jnp.concatenate([a_pool, o1_ref[...] * (e1 / z), o2_ref[...] * (e2 / z), o3_ref[...] * (e3 / z)],
                          axis=-1).astype(BF16)
    mix = jnp.dot(cat, wout_ref[...], preferred_element_type=F32)
    out_ref[...] = x_ref[...] + _rmsnorm(mix, gpost_ref[...])


def _mix_out(x, u, attn, w_pool_bd, pool_scale, w_out, g_post, *, tm, seq):
    n = x.shape[0]
    n_halo = n // POOL_HALO
    halo_per_tm = tm // POOL_HALO
    row = lambda width: pl.BlockSpec((tm, width), lambda i: (i, 0))
    prev = pl.BlockSpec((POOL_HALO, POOL_DIM), lambda i: (jnp.maximum(i * halo_per_tm - 1, 0), 0))
    nxt = pl.BlockSpec((POOL_HALO, POOL_DIM), lambda i: (jnp.minimum((i + 1) * halo_per_tm, n_halo - 1), 0))
    (o1, l1), (o2, l2), (o3, l3) = attn
    buf = pltpu.VMEM((tm + 3 * POOL_HALO, POOL_DIM), F32)
    return pl.pallas_call(
        functools.partial(_mix_out_kernel, tm=tm, seq=seq),
        grid=(n // tm,),
        in_specs=[row(D_MODEL), row(POOL_DIM), prev, nxt] + [row(GROUP_DIM)] * 6 +
                 [_const_spec((POOL_DIM, POOL_DIM)), _const_spec((1, POOL_DIM)),
                  _const_spec((D_MODEL, D_MODEL)), _const_spec((1, D_MODEL))],
        out_specs=row(D_MODEL),
        out_shape=jax.ShapeDtypeStruct((n, D_MODEL), F32),
        scratch_shapes=[buf, buf],
        compiler_params=_params(1),
        name="mix_out",
    )(x, u, u, u, o1, o2, o3, l1, l2, l3, w_pool_bd, pool_scale, w_out, g_post)


def _ffn_weights(w_gate, w_up, w_down):
    pad = FF_PAD - D_FF
    return (jnp.pad(w_gate.astype(BF16), ((0, 0), (0, pad))),
            jnp.pad(w_up.astype(BF16), ((0, 0), (0, pad))),
            jnp.pad(w_down.astype(BF16), ((0, pad), (0, 0))))


def _block_diag(w_lin):
    g, c, e = w_lin.shape
    eye = jnp.eye(g, dtype=w_lin.dtype)
    return (eye[:, None, :, None] * w_lin[:, :, None, :]).reshape(g * c, g * e)


def kernel(x, g_ffn1_pre, w1_gate, w1_up, w1_down, g_ffn1_post, g_mix_pre, w_in, w_pool_lin, pool_scale,
           w_out, g_mix_post, g_ffn2_pre, w2_gate, w2_up, w2_down, g_ffn2_post):
    batch, seq, _ = x.shape
    depth = g_ffn1_pre.shape[0]
    tm = 512
    h = x.reshape(batch * seq, D_MODEL)
    for l in range(depth):
        h = _ffn(h, g_ffn1_pre[l][None], *_ffn_weights(w1_gate[l], w1_up[l], w1_down[l]),
                 g_ffn1_post[l][None], tm=tm, ff_chunk=V7X_MXU_DIM)
        w_in_bf = w_in[l].astype(BF16)
        u, qkv = _proj_in(h, g_mix_pre[l][None], w_in_bf[:, :POOL_DIM], w_in_bf[:, POOL_DIM:], tm=tm)
        attn = [_attention_group(qkv, gi, dilation, batch, seq)
                for gi, (_, dilation) in enumerate(ATTN_GROUPS)]
        h = _mix_out(h, u, attn, _block_diag(w_pool_lin[l]).astype(BF16), pool_scale[l][None],
                     w_out[l].astype(BF16), g_mix_post[l][None], tm=tm, seq=seq)
        h = _ffn(h, g_ffn2_pre[l][None], *_ffn_weights(w2_gate[l], w2_up[l], w2_down[l]),
                 g_ffn2_post[l][None], tm=tm, ff_chunk=V7X_MXU_DIM)
    return h.reshape(batch, seq, D_MODEL)
```

```python
import functools

import numpy as np
import jax
import jax.numpy as jnp
from jax import lax
from jax.experimental import pallas as pl
from jax.experimental.pallas import tpu as pltpu

D_MODEL = 1024
HEAD_DIM = 64
POOL_WINDOWS = (2, 4, 8, 16)
POOL_GROUP_DIM = 64
POOL_DIM = len(POOL_WINDOWS) * POOL_GROUP_DIM
ATTN_GROUPS = ((128, 1), (512, 4), (2048, 16))
HEADS_PER_GROUP = 4
N_ATTN_HEADS = HEADS_PER_GROUP * len(ATTN_GROUPS)
GROUP_DIM = HEADS_PER_GROUP * HEAD_DIM
ATTN_DIM = N_ATTN_HEADS * HEAD_DIM
D_QKV = 3 * ATTN_DIM
D_FF = 2752
RMS_EPS = 1e-6

QKV_GROUP_DIM = 3 * GROUP_DIM

LANES = 128
V7X_MXU_DIM = 256
V7X_VMEM_LIMIT_BYTES = 56 * 1024 * 1024
FF_PAD = -(-D_FF // V7X_MXU_DIM) * V7X_MXU_DIM
N_SIDE = 64
Q_SUB = 128
K_WIN = Q_SUB + 2 * N_SIDE
POOL_HALO = 8
MASKED = -1e30

F32 = jnp.float32
BF16 = jnp.bfloat16


def _rmsnorm(x, g):
    ms = jnp.mean(x * x, axis=-1, keepdims=True)
    return x * lax.rsqrt(ms + RMS_EPS) * g


def _const_spec(shape):
    return pl.BlockSpec(shape, lambda *_: (0,) * len(shape), pipeline_mode=pl.Buffered(1))


def _params(n_axes):
    return pltpu.CompilerParams(dimension_semantics=("arbitrary",) * n_axes,
                                vmem_limit_bytes=V7X_VMEM_LIMIT_BYTES)


def _ffn_kernel(x_ref, gpre_ref, wg_ref, wu_ref, wd_ref, gpost_ref, o_ref, acc_ref, *, ff_chunk):
    x = x_ref[...]
    xn = _rmsnorm(x, gpre_ref[...]).astype(BF16)
    for c0 in range(0, FF_PAD, ff_chunk):
        gate = jnp.dot(xn, wg_ref[:, c0:c0 + ff_chunk], preferred_element_type=F32)
        up = jnp.dot(xn, wu_ref[:, c0:c0 + ff_chunk], preferred_element_type=F32)
        act = (gate * jax.nn.sigmoid(gate) * up).astype(BF16)
        part = jnp.dot(act, wd_ref[c0:c0 + ff_chunk, :], preferred_element_type=F32)
        if c0 == 0:
            acc_ref[...] = part
        else:
            acc_ref[...] += part
    o_ref[...] = x + 0.5 * _rmsnorm(acc_ref[...], gpost_ref[...])


def _ffn(x, g_pre, wg, wu, wd, g_post, *, tm, ff_chunk):
    n = x.shape[0]
    row = pl.BlockSpec((tm, D_MODEL), lambda i: (i, 0))
    return pl.pallas_call(
        functools.partial(_ffn_kernel, ff_chunk=ff_chunk),
        grid=(n // tm,),
        in_specs=[row, _const_spec((1, D_MODEL)), _const_spec((D_MODEL, FF_PAD)),
                  _const_spec((D_MODEL, FF_PAD)), _const_spec((FF_PAD, D_MODEL)),
                  _const_spec((1, D_MODEL))],
        out_specs=row,
        out_shape=jax.ShapeDtypeStruct((n, D_MODEL), F32),
        scratch_shapes=[pltpu.VMEM((tm, D_MODEL), F32)],
        compiler_params=_params(1),
        name="ffn",
    )(x, g_pre, wg, wu, wd, g_post)


def _proj_in_kernel(x_ref, g_ref, wu_ref, wqkv_ref, u_ref, *rest, tm):
    qkv_refs, zbuf = rest[:-1], rest[-1]
    h = _rmsnorm(x_ref[...], g_ref[...]).astype(BF16)
    u_ref[...] = jnp.dot(h, wu_ref[...], preferred_element_type=F32)
    for gi, (_, dilation) in enumerate(ATTN_GROUPS):
        z = jnp.dot(h, wqkv_ref[:, gi * QKV_GROUP_DIM:(gi + 1) * QKV_GROUP_DIM], preferred_element_type=F32)
        out = qkv_refs[gi]
        if dilation == 1:
            out[0] = z.astype(BF16)
            continue
        for s in range(QKV_GROUP_DIM // LANES):
            zbuf[s] = z[:, s * LANES:(s + 1) * LANES]
        rows = tm // dilation
        for r in range(dilation):
            for s in range(QKV_GROUP_DIM // LANES):
                out[r, :, s * LANES:(s + 1) * LANES] = zbuf[s, pl.ds(r, rows, stride=dilation), :].astype(BF16)


def _proj_in(x, g, w_pool_in, w_qkv, *, tm):
    batch, seq, _ = x.shape
    qkv_specs = [pl.BlockSpec((None, d, tm // d, QKV_GROUP_DIM), lambda b, i: (b, 0, i, 0)) for _, d in ATTN_GROUPS]
    qkv_shapes = [jax.ShapeDtypeStruct((batch, d, seq // d, QKV_GROUP_DIM), BF16) for _, d in ATTN_GROUPS]
    return pl.pallas_call(
        functools.partial(_proj_in_kernel, tm=tm),
        grid=(batch, seq // tm),
        in_specs=[pl.BlockSpec((None, tm, D_MODEL), lambda b, i: (b, i, 0)), _const_spec((1, D_MODEL)),
                  _const_spec((D_MODEL, POOL_DIM)), _const_spec((D_MODEL, D_QKV))],
        out_specs=[pl.BlockSpec((None, tm, POOL_DIM), lambda b, i: (b, i, 0))] + qkv_specs,
        out_shape=[jax.ShapeDtypeStruct((batch, seq, POOL_DIM), F32)] + qkv_shapes,
        scratch_shapes=[pltpu.VMEM((QKV_GROUP_DIM // LANES, tm, LANES), F32)],
        compiler_params=_params(2),
        name="proj_in",
    )(x, g, w_pool_in, w_qkv)


def _attn_kernel(q_ref, kp_ref, kc_ref, kn_ref, vp_ref, vc_ref, vn_ref, o_ref, lse_ref,
                 kwin, vwin, qs_ref, bias_ref, *, dilation, slopes, sub_len, tq):
    first = (pl.program_id(0) == 0) & (pl.program_id(1) == 0) & (pl.program_id(2) == 0)

    @pl.when(first)
    def _():
        qi = lax.broadcasted_iota(jnp.int32, (Q_SUB, K_WIN), 0)
        kj = lax.broadcasted_iota(jnp.int32, (Q_SUB, K_WIN), 1)
        rel = jnp.abs(kj - N_SIDE - qi)
        dist = (rel * dilation).astype(F32)
        in_band = rel <= N_SIDE
        for variant in range(4):
            ok = in_band
            if variant & 1:
                ok = ok & (kj >= N_SIDE)
            if variant & 2:
                ok = ok & (kj < N_SIDE + Q_SUB)
            for h in range(HEADS_PER_GROUP):
                bias_ref[variant, h * Q_SUB:(h + 1) * Q_SUB, :] = jnp.where(ok, -slopes[h] * dist, MASKED)

    kwin[0:N_SIDE] = kp_ref[...]
    kwin[N_SIDE:N_SIDE + tq] = kc_ref[...]
    kwin[N_SIDE + tq:] = kn_ref[...]
    vwin[0:N_SIDE] = vp_ref[...]
    vwin[N_SIDE:N_SIDE + tq] = vc_ref[...]
    vwin[N_SIDE + tq:] = vn_ref[...]

    n_sub = tq // Q_SUB
    sub0 = pl.program_id(2) * n_sub
    last_sub = sub_len // Q_SUB - 1
    lane = lax.broadcasted_iota(jnp.int32, (1, GROUP_DIM), 1)
    head_of_lane = lane >> 6
    scale = HEAD_DIM ** -0.5
    even_head = lax.broadcasted_iota(jnp.int32, (1, 128), 1) < HEAD_DIM

    def sub_block(j, carry):
        r0 = pl.multiple_of(j * Q_SUB, Q_SUB)
        q = q_ref[pl.ds(r0, Q_SUB), :]
        for h in range(HEADS_PER_GROUP):
            qs_ref[h * Q_SUB:(h + 1) * Q_SUB, :] = q * jnp.where(head_of_lane == h, scale, 0.0).astype(BF16)
        kw = kwin[pl.ds(r0, K_WIN), :]
        vw = vwin[pl.ds(r0, K_WIN), :]
        g = sub0 + j
        variant = (g == 0).astype(jnp.int32) + 2 * (g == last_sub).astype(jnp.int32)
        s = lax.dot_general(qs_ref[...], kw, (((1,), (1,)), ((), ())), preferred_element_type=F32)
        s = s + bias_ref[variant]
        m = jnp.max(s, axis=-1, keepdims=True)
        p = jnp.exp(s - m)
        den = jnp.sum(p, axis=-1, keepdims=True)
        o_all = jnp.dot(p.astype(BF16), vw, preferred_element_type=F32)
        lse = m + jnp.log(den)
        for half in range(2):
            cols = slice(half * 128, (half + 1) * 128)
            ra = slice(2 * half * Q_SUB, (2 * half + 1) * Q_SUB)
            rb = slice((2 * half + 1) * Q_SUB, (2 * half + 2) * Q_SUB)
            o_half = jnp.where(even_head, o_all[ra, cols], o_all[rb, cols])
            den_half = jnp.where(even_head, den[ra], den[rb])
            o_ref[pl.ds(r0, Q_SUB), cols] = o_half / den_half
            lse_ref[pl.ds(r0, Q_SUB), cols] = jnp.where(even_head, lse[ra], lse[rb])
        return carry

    lax.fori_loop(0, n_sub, sub_block, 0, unroll=True)


def _attention_group(qkv, group):
    batch, dilation, sub_len, _ = qkv.shape
    tq = min(512, sub_len)
    assert sub_len % tq == 0 and tq % Q_SUB == 0
    halo_per_tq = tq // N_SIDE
    n_halo = sub_len // N_SIDE
    slopes = tuple(float(np.float32(2.0 ** (-8.0 * (group * HEADS_PER_GROUP + h + 1) / N_ATTN_HEADS)))
                   for h in range(HEADS_PER_GROUP))

    def cur(part):
        return pl.BlockSpec((None, None, tq, GROUP_DIM), lambda b, r, i: (b, r, i, part))

    def prev(part):
        return pl.BlockSpec((None, None, N_SIDE, GROUP_DIM),
                            lambda b, r, i: (b, r, jnp.maximum(i * halo_per_tq - 1, 0), part))

    def nxt(part):
        return pl.BlockSpec((None, None, N_SIDE, GROUP_DIM),
                            lambda b, r, i: (b, r, jnp.minimum((i + 1) * halo_per_tq, n_halo - 1), part))

    out_spec = pl.BlockSpec((None, None, tq, GROUP_DIM), lambda b, r, i: (b, r, i, 0))
    out_shape = jax.ShapeDtypeStruct((batch, dilation, sub_len, GROUP_DIM), F32)
    return pl.pallas_call(
        functools.partial(_attn_kernel, dilation=dilation, slopes=slopes, sub_len=sub_len, tq=tq),
        grid=(batch, dilation, sub_len // tq),
        in_specs=[cur(0), prev(1), cur(1), nxt(1), prev(2), cur(2), nxt(2)],
        out_specs=[out_spec, out_spec],
        out_shape=[out_shape, out_shape],
        scratch_shapes=[pltpu.VMEM((tq + 2 * N_SIDE, GROUP_DIM), BF16),
                        pltpu.VMEM((tq + 2 * N_SIDE, GROUP_DIM), BF16),
                        pltpu.VMEM((HEADS_PER_GROUP * Q_SUB, GROUP_DIM), BF16),
                        pltpu.VMEM((4, HEADS_PER_GROUP * Q_SUB, K_WIN), F32)],
        compiler_params=_params(3),
        name=f"attn_d{dilation}",
    )(qkv, qkv, qkv, qkv, qkv, qkv, qkv)


def _pool_mean_minus_token(u_ref, up_ref, un_ref, ubuf, abuf, *, tm, seq):
    tiles_per_seq = seq // tm
    ti = pl.program_id(1)
    n = tm + 2 * POOL_HALO
    zero_halo = jnp.zeros((POOL_HALO, POOL_DIM), F32)
    ubuf[0:POOL_HALO] = jnp.where(ti == 0, zero_halo, up_ref[...])
    ubuf[POOL_HALO:POOL_HALO + tm] = u_ref[...]
    ubuf[POOL_HALO + tm:n] = jnp.where(ti == tiles_per_seq - 1, zero_halo, un_ref[...])
    ubuf[n:] = zero_halo
    abuf[n:] = zero_halo
    a2 = ubuf[0:n] + ubuf[1:n + 1]
    abuf[0:n] = a2
    a4 = a2 + abuf[2:n + 2]
    ubuf[0:n] = a4
    a8 = a4 + ubuf[4:n + 4]
    abuf[0:n] = a8
    a16 = a8 + abuf[8:n + 8]
    ubuf[0:n] = a2
    s2 = ubuf[7:7 + tm]
    ubuf[0:n] = a4
    s4 = ubuf[6:6 + tm]
    s8 = abuf[4:4 + tm]
    s16 = a16[0:tm]
    lane = lax.broadcasted_iota(jnp.int32, (1, POOL_DIM), 1)
    grp = lane >> 6
    win_sum = jnp.where(grp == 0, s2, jnp.where(grp == 1, s4, jnp.where(grp == 2, s8, s16)))
    t = ti * tm + lax.broadcasted_iota(jnp.int32, (tm, 1), 0)
    half = jnp.where(grp == 0, 1, jnp.where(grp == 1, 2, jnp.where(grp == 2, 4, 8)))
    count = jnp.minimum(t + half, seq) - jnp.maximum(t - half, 0)
    return win_sum / count.astype(F32) - u_ref[...]


def _token_order(blk_ref, buf, *, tm):
    dilation = blk_ref.shape[0]
    if dilation == 1:
        return blk_ref[0]
    rows = tm // dilation
    n_slabs = GROUP_DIM // LANES
    for r in range(dilation):
        for s in range(n_slabs):
            buf[s, pl.ds(r, rows, stride=dilation), :] = blk_ref[r, :, s * LANES:(s + 1) * LANES]
    return jnp.concatenate([buf[s] for s in range(n_slabs)], axis=-1)


def _mix_out_kernel(x_ref, u_ref, up_ref, un_ref, o1_ref, o2_ref, o3_ref, l1_ref, l2_ref, l3_ref,
                    wpool_ref, pscale_ref, wout_ref, gpost_ref, out_ref, ubuf, abuf, ob2, ob3, lb2, lb3,
                    *, tm, seq):
    y = _pool_mean_minus_token(u_ref, up_ref, un_ref, ubuf, abuf, tm=tm, seq=seq)
    a_pool = jnp.dot(y.astype(BF16), wpool_ref[...], preferred_element_type=F32) * pscale_ref[...]
    l1 = l1_ref[0]
    l2 = _token_order(l2_ref, lb2, tm=tm)
    l3 = _token_order(l3_ref, lb3, tm=tm)
    m = jnp.maximum(jnp.maximum(l1, l2), l3)
    e1, e2, e3 = jnp.exp(l1 - m), jnp.exp(l2 - m), jnp.exp(l3 - m)
    z = e1 + e2 + e3
    o2 = _token_order(o2_ref, ob2, tm=tm)
    o3 = _token_order(o3_ref, ob3, tm=tm)
    cat = jnp.concatenate([a_pool, o1_ref[0] * (e1 / z), o2 * (e2 / z), o3 * (e3 / z)], axis=-1).astype(BF16)
    mix = jnp.dot(cat, wout_ref[...], preferred_element_type=F32)
    out_ref[...] = x_ref[...] + _rmsnorm(mix, gpost_ref[...])


def _mix_out(x, u, attn, w_pool_bd, pool_scale, w_out, g_post, *, tm):
    batch, seq, _ = x.shape
    n_halo = seq // POOL_HALO
    halo_per_tm = tm // POOL_HALO
    row = lambda width: pl.BlockSpec((None, tm, width), lambda b, i: (b, i, 0))
    prev = pl.BlockSpec((None, POOL_HALO, POOL_DIM), lambda b, i: (b, jnp.maximum(i * halo_per_tm - 1, 0), 0))
    nxt = pl.BlockSpec((None, POOL_HALO, POOL_DIM),
                       lambda b, i: (b, jnp.minimum((i + 1) * halo_per_tm, n_halo - 1), 0))
    by_residue = [pl.BlockSpec((None, d, tm // d, GROUP_DIM), lambda b, i: (b, 0, i, 0)) for _, d in ATTN_GROUPS]
    (o1, l1), (o2, l2), (o3, l3) = attn
    pool_buf = pltpu.VMEM((tm + 3 * POOL_HALO, POOL_DIM), F32)
    order_buf = pltpu.VMEM((GROUP_DIM // LANES, tm, LANES), F32)
    return pl.pallas_call(
        functools.partial(_mix_out_kernel, tm=tm, seq=seq),
        grid=(batch, seq // tm),
        in_specs=[row(D_MODEL), row(POOL_DIM), prev, nxt] + by_residue + by_residue +
                 [_const_spec((POOL_DIM, POOL_DIM)), _const_spec((1, POOL_DIM)),
                  _const_spec((D_MODEL, D_MODEL)), _const_spec((1, D_MODEL))],
        out_specs=row(D_MODEL),
        out_shape=jax.ShapeDtypeStruct((batch, seq, D_MODEL), F32),
        scratch_shapes=[pool_buf, pool_buf, order_buf, order_buf, order_buf, order_buf],
        compiler_params=_params(2),
        name="mix_out",
    )(x, u, u, u, o1, o2, o3, l1, l2, l3, w_pool_bd, pool_scale, w_out, g_post)


def _ffn_weights(w_gate, w_up, w_down):
    pad = FF_PAD - D_FF
    return (jnp.pad(w_gate.astype(BF16), ((0, 0), (0, pad))),
            jnp.pad(w_up.astype(BF16), ((0, 0), (0, pad))),
            jnp.pad(w_down.astype(BF16), ((0, pad), (0, 0))))


def _block_diag(w_lin):
    g, c, e = w_lin.shape
    eye = jnp.eye(g, dtype=w_lin.dtype)
    return (eye[:, None, :, None] * w_lin[:, :, None, :]).reshape(g * c, g * e)


def _qkv_by_group(w_qkv):
    d_model = w_qkv.shape[0]
    n_groups = len(ATTN_GROUPS)
    return w_qkv.reshape(d_model, 3, n_groups, GROUP_DIM).transpose(0, 2, 1, 3).reshape(d_model, D_QKV)


def kernel(x, g_ffn1_pre, w1_gate, w1_up, w1_down, g_ffn1_post, g_mix_pre, w_in, w_pool_lin, pool_scale,
           w_out, g_mix_post, g_ffn2_pre, w2_gate, w2_up, w2_down, g_ffn2_post):
    batch, seq, _ = x.shape
    depth = g_ffn1_pre.shape[0]
    tm = 512
    assert seq % tm == 0 and tm % (ATTN_GROUPS[-1][1] * 16) == 0
    h = x
    for l in range(depth):
        h = _ffn(h.reshape(batch * seq, D_MODEL), g_ffn1_pre[l][None],
                 *_ffn_weights(w1_gate[l], w1_up[l], w1_down[l]), g_ffn1_post[l][None],
                 tm=tm, ff_chunk=V7X_MXU_DIM).reshape(batch, seq, D_MODEL)
        w_in_bf = w_in[l].astype(BF16)
        u, *qkv = _proj_in(h, g_mix_pre[l][None], w_in_bf[:, :POOL_DIM], _qkv_by_group(w_in_bf[:, POOL_DIM:]), tm=tm)
        attn = [_attention_group(qkv[gi], gi) for gi in range(len(ATTN_GROUPS))]
        h = _mix_out(h, u, attn, _block_diag(w_pool_lin[l]).astype(BF16), pool_scale[l][None],
                     w_out[l].astype(BF16), g_mix_post[l][None], tm=tm)
        h = _ffn(h.reshape(batch * seq, D_MODEL), g_ffn2_pre[l][None],
                 *_ffn_weights(w2_gate[l], w2_up[l], w2_down[l]), g_ffn2_post[l][None],
                 tm=tm, ff_chunk=V7X_MXU_DIM).reshape(batch, seq, D_MODEL)
    return h
```

```python
import functools

import numpy as np
import jax
import jax.numpy as jnp
from jax import lax
from jax.experimental import pallas as pl
from jax.experimental.pallas import tpu as pltpu

D_MODEL = 1024
HEAD_DIM = 64
POOL_WINDOWS = (2, 4, 8, 16)
POOL_GROUP_DIM = 64
POOL_DIM = len(POOL_WINDOWS) * POOL_GROUP_DIM
ATTN_GROUPS = ((128, 1), (512, 4), (2048, 16))
HEADS_PER_GROUP = 4
N_ATTN_HEADS = HEADS_PER_GROUP * len(ATTN_GROUPS)
GROUP_DIM = HEADS_PER_GROUP * HEAD_DIM
ATTN_DIM = N_ATTN_HEADS * HEAD_DIM
D_QKV = 3 * ATTN_DIM
QKV_GROUP_DIM = 3 * GROUP_DIM
D_FF = 2752
RMS_EPS = 1e-6

LANES = 128
V7X_MXU_DIM = 256
V7X_VMEM_LIMIT_BYTES = 56 * 1024 * 1024
FF_PAD = -(-D_FF // V7X_MXU_DIM) * V7X_MXU_DIM
N_SIDE = 64
Q_SUB = 128
K_WIN = Q_SUB + 2 * N_SIDE
ATTN_ROWS_PER_STEP = 1024
POOL_HALO = 8
MASKED = -1e30

F32 = jnp.float32
BF16 = jnp.bfloat16


def _rmsnorm(x, g):
    ms = jnp.mean(x * x, axis=-1, keepdims=True)
    return x * lax.rsqrt(ms + RMS_EPS) * g


def _const_spec(shape):
    return pl.BlockSpec(shape, lambda *_: (0,) * len(shape), pipeline_mode=pl.Buffered(1))


def _params(n_axes):
    return pltpu.CompilerParams(dimension_semantics=("arbitrary",) * n_axes,
                                vmem_limit_bytes=V7X_VMEM_LIMIT_BYTES)


def _ffn_kernel(x_ref, gpre_ref, wg_ref, wu_ref, wd_ref, gpost_ref, o_ref, acc_ref, *, ff_chunk):
    x = x_ref[...]
    xn = _rmsnorm(x, gpre_ref[...]).astype(BF16)
    for c0 in range(0, FF_PAD, ff_chunk):
        c1 = min(c0 + ff_chunk, FF_PAD)
        gate = jnp.dot(xn, wg_ref[:, c0:c1], preferred_element_type=F32)
        up = jnp.dot(xn, wu_ref[:, c0:c1], preferred_element_type=F32)
        act = (gate * jax.nn.sigmoid(gate) * up).astype(BF16)
        part = jnp.dot(act, wd_ref[c0:c1, :], preferred_element_type=F32)
        if c0 == 0:
            acc_ref[...] = part
        else:
            acc_ref[...] += part
    o_ref[...] = x + 0.5 * _rmsnorm(acc_ref[...], gpost_ref[...])


def _ffn(x, g_pre, wg, wu, wd, g_post, *, tm, ff_chunk):
    n = x.shape[0]
    row = pl.BlockSpec((tm, D_MODEL), lambda i: (i, 0))
    return pl.pallas_call(
        functools.partial(_ffn_kernel, ff_chunk=ff_chunk),
        grid=(n // tm,),
        in_specs=[row, _const_spec((1, D_MODEL)), _const_spec((D_MODEL, FF_PAD)),
                  _const_spec((D_MODEL, FF_PAD)), _const_spec((FF_PAD, D_MODEL)),
                  _const_spec((1, D_MODEL))],
        out_specs=row,
        out_shape=jax.ShapeDtypeStruct((n, D_MODEL), F32),
        scratch_shapes=[pltpu.VMEM((tm, D_MODEL), F32)],
        compiler_params=_params(1),
        name="ffn",
    )(x, g_pre, wg, wu, wd, g_post)


def _proj_in_kernel(x_ref, g_ref, wu_ref, wqkv_ref, u_ref, *rest, tm):
    qkv_refs, hbuf, zbuf = rest[:-2], rest[-2], rest[-1]
    hbuf[...] = _rmsnorm(x_ref[...], g_ref[...]).astype(BF16)
    u_ref[...] = jnp.dot(hbuf[...], wu_ref[...], preferred_element_type=F32)
    for gi, (_, dilation) in reversed(list(enumerate(ATTN_GROUPS))):
        z = jnp.dot(hbuf[...], wqkv_ref[:, gi * QKV_GROUP_DIM:(gi + 1) * QKV_GROUP_DIM],
                    preferred_element_type=F32)
        out = qkv_refs[gi]
        if dilation == 1:
            out[0] = z.astype(BF16)
            continue
        for s in range(QKV_GROUP_DIM // LANES):
            zbuf[gi - 1, s] = z[:, s * LANES:(s + 1) * LANES]
        rows = tm // dilation
        for r in range(dilation):
            for s in range(QKV_GROUP_DIM // LANES):
                out[r, :, s * LANES:(s + 1) * LANES] = (
                    zbuf[gi - 1, s, pl.ds(r, rows, stride=dilation), :].astype(BF16))


def _proj_in(x, g, w_pool_in, w_qkv, *, tm):
    batch, seq, _ = x.shape
    qkv_specs = [pl.BlockSpec((None, d, tm // d, QKV_GROUP_DIM), lambda b, i: (b, 0, i, 0)) for _, d in ATTN_GROUPS]
    qkv_shapes = [jax.ShapeDtypeStruct((batch, d, seq // d, QKV_GROUP_DIM), BF16) for _, d in ATTN_GROUPS]
    return pl.pallas_call(
        functools.partial(_proj_in_kernel, tm=tm),
        grid=(batch, seq // tm),
        in_specs=[pl.BlockSpec((None, tm, D_MODEL), lambda b, i: (b, i, 0)), _const_spec((1, D_MODEL)),
                  _const_spec((D_MODEL, POOL_DIM)), _const_spec((D_MODEL, D_QKV))],
        out_specs=[pl.BlockSpec((None, tm, POOL_DIM), lambda b, i: (b, i, 0))] + qkv_specs,
        out_shape=[jax.ShapeDtypeStruct((batch, seq, POOL_DIM), F32)] + qkv_shapes,
        scratch_shapes=[pltpu.VMEM((tm, D_MODEL), BF16),
                        pltpu.VMEM((len(ATTN_GROUPS) - 1, QKV_GROUP_DIM // LANES, tm, LANES), F32)],
        compiler_params=_params(2),
        name="proj_in",
    )(x, g, w_pool_in, w_qkv)


def _attn_kernel(q_ref, kp_ref, kc_ref, kn_ref, vp_ref, vc_ref, vn_ref, o_ref, lse_ref,
                 kwin, vwin, qs_ref, bias_ref, *, dilation, slopes, sub_len, tq, n_res):
    first = (pl.program_id(0) == 0) & (pl.program_id(1) == 0) & (pl.program_id(2) == 0)

    @pl.when(first)
    def _():
        qi = lax.broadcasted_iota(jnp.int32, (Q_SUB, K_WIN), 0)
        kj = lax.broadcasted_iota(jnp.int32, (Q_SUB, K_WIN), 1)
        rel = jnp.abs(kj - N_SIDE - qi)
        dist = (rel * dilation).astype(F32)
        in_band = rel <= N_SIDE
        for variant in range(4):
            ok = in_band
            if variant & 1:
                ok = ok & (kj >= N_SIDE)
            if variant & 2:
                ok = ok & (kj < N_SIDE + Q_SUB)
            for h in range(HEADS_PER_GROUP):
                bias_ref[variant, h * Q_SUB:(h + 1) * Q_SUB, :] = jnp.where(ok, -slopes[h] * dist, MASKED)

    kwin[:, 0:N_SIDE] = kp_ref[...]
    kwin[:, N_SIDE:N_SIDE + tq] = kc_ref[...]
    kwin[:, N_SIDE + tq:] = kn_ref[...]
    vwin[:, 0:N_SIDE] = vp_ref[...]
    vwin[:, N_SIDE:N_SIDE + tq] = vc_ref[...]
    vwin[:, N_SIDE + tq:] = vn_ref[...]

    n_sub = tq // Q_SUB
    sub0 = pl.program_id(2) * n_sub
    last_sub = sub_len // Q_SUB - 1
    lane = lax.broadcasted_iota(jnp.int32, (1, GROUP_DIM), 1)
    head_of_lane = lane >> 6
    scale = HEAD_DIM ** -0.5
    even_head = lax.broadcasted_iota(jnp.int32, (1, LANES), 1) < HEAD_DIM

    for res in range(n_res):
        for j in range(n_sub):
            rows = slice(j * Q_SUB, (j + 1) * Q_SUB)
            q = q_ref[res, rows, :]
            for h in range(HEADS_PER_GROUP):
                qs_ref[h * Q_SUB:(h + 1) * Q_SUB, :] = q * jnp.where(head_of_lane == h, scale, 0.0).astype(BF16)
            kw = kwin[res, j * Q_SUB:j * Q_SUB + K_WIN, :]
            vw = vwin[res, j * Q_SUB:j * Q_SUB + K_WIN, :]
            g = sub0 + j
            variant = (g == 0).astype(jnp.int32) + 2 * (g == last_sub).astype(jnp.int32)
            s = lax.dot_general(qs_ref[...], kw, (((1,), (1,)), ((), ())), preferred_element_type=F32)
            s = s + bias_ref[variant]
            m = jnp.max(s, axis=-1, keepdims=True)
            p = jnp.exp(s - m)
            den = jnp.sum(p, axis=-1, keepdims=True)
            pb = p.astype(BF16)
            p_wide = jnp.concatenate([pb[h * Q_SUB:(h + 1) * Q_SUB] for h in range(HEADS_PER_GROUP)], axis=-1)
            v_tall = jnp.concatenate([vw * (head_of_lane == h).astype(BF16) for h in range(HEADS_PER_GROUP)], axis=0)
            o = jnp.dot(p_wide, v_tall, preferred_element_type=F32)
            lse = m + jnp.log(den)
            for half in range(2):
                cols = slice(half * LANES, (half + 1) * LANES)
                ra = slice(2 * half * Q_SUB, (2 * half + 1) * Q_SUB)
                rb = slice((2 * half + 1) * Q_SUB, (2 * half + 2) * Q_SUB)
                den_half = jnp.where(even_head, den[ra], den[rb])
                o_ref[res, rows, cols] = o[:, cols] / den_half
                lse_ref[res, rows, cols] = jnp.where(even_head, lse[ra], lse[rb])


def _attention_group(qkv, group):
    batch, dilation, sub_len, _ = qkv.shape
    tq = min(ATTN_ROWS_PER_STEP, sub_len)
    n_res = min(ATTN_ROWS_PER_STEP // tq, dilation)
    assert sub_len % tq == 0 and tq % Q_SUB == 0 and dilation % n_res == 0
    halo_per_tq = tq // N_SIDE
    n_halo = sub_len // N_SIDE
    slopes = tuple(float(np.float32(2.0 ** (-8.0 * (group * HEADS_PER_GROUP + h + 1) / N_ATTN_HEADS)))
                   for h in range(HEADS_PER_GROUP))

    def cur(part):
        return pl.BlockSpec((None, n_res, tq, GROUP_DIM), lambda b, r, i: (b, r, i, part))

    def prev(part):
        return pl.BlockSpec((None, n_res, N_SIDE, GROUP_DIM),
                            lambda b, r, i: (b, r, jnp.maximum(i * halo_per_tq - 1, 0), part))

    def nxt(part):
        return pl.BlockSpec((None, n_res, N_SIDE, GROUP_DIM),
                            lambda b, r, i: (b, r, jnp.minimum((i + 1) * halo_per_tq, n_halo - 1), part))

    out_spec = pl.BlockSpec((None, n_res, tq, GROUP_DIM), lambda b, r, i: (b, r, i, 0))
    out_shape = jax.ShapeDtypeStruct((batch, dilation, sub_len, GROUP_DIM), F32)
    return pl.pallas_call(
        functools.partial(_attn_kernel, dilation=dilation, slopes=slopes, sub_len=sub_len, tq=tq, n_res=n_res),
        grid=(batch, dilation // n_res, sub_len // tq),
        in_specs=[cur(0), prev(1), cur(1), nxt(1), prev(2), cur(2), nxt(2)],
        out_specs=[out_spec, out_spec],
        out_shape=[out_shape, out_shape],
        scratch_shapes=[pltpu.VMEM((n_res, tq + 2 * N_SIDE, GROUP_DIM), BF16),
                        pltpu.VMEM((n_res, tq + 2 * N_SIDE, GROUP_DIM), BF16),
                        pltpu.VMEM((HEADS_PER_GROUP * Q_SUB, GROUP_DIM), BF16),
                        pltpu.VMEM((4, HEADS_PER_GROUP * Q_SUB, K_WIN), F32)],
        compiler_params=_params(3),
        name=f"attn_d{dilation}",
    )(qkv, qkv, qkv, qkv, qkv, qkv, qkv)


def _pool_mean_minus_token(u_ref, up_ref, un_ref, ubuf, abuf, *, tm, seq):
    tiles_per_seq = seq // tm
    ti = pl.program_id(1)
    n = tm + 2 * POOL_HALO
    zero_halo = jnp.zeros((POOL_HALO, POOL_DIM), F32)
    ubuf[0:POOL_HALO] = jnp.where(ti == 0, zero_halo, up_ref[...])
    ubuf[POOL_HALO:POOL_HALO + tm] = u_ref[...]
    ubuf[POOL_HALO + tm:n] = jnp.where(ti == tiles_per_seq - 1, zero_halo, un_ref[...])
    ubuf[n:] = zero_halo
    abuf[n:] = zero_halo
    lo, hi = slice(0, LANES), slice(LANES, 2 * LANES)
    a2 = ubuf[0:n, lo] + ubuf[1:n + 1, lo]
    abuf[0:n, lo] = a2
    s2 = abuf[7:7 + tm, lo]
    ubuf[0:n, lo] = a2 + abuf[2:n + 2, lo]
    s4 = ubuf[6:6 + tm, lo]
    a2 = ubuf[0:n, hi] + ubuf[1:n + 1, hi]
    abuf[0:n, hi] = a2
    a4 = a2 + abuf[2:n + 2, hi]
    ubuf[0:n, hi] = a4
    a8 = a4 + ubuf[4:n + 4, hi]
    abuf[0:n, hi] = a8
    s8 = abuf[4:4 + tm, hi]
    s16 = a8[0:tm] + abuf[8:8 + tm, hi]
    first_group = lax.broadcasted_iota(jnp.int32, (1, LANES), 1) < POOL_GROUP_DIM
    t = ti * tm + lax.broadcasted_iota(jnp.int32, (tm, 1), 0)

    def mean(sum_a, sum_b, half_a, half_b):
        half = jnp.where(first_group, half_a, half_b)
        count = jnp.minimum(t + half, seq) - jnp.maximum(t - half, 0)
        return jnp.where(first_group, sum_a, sum_b) / count.astype(F32)

    return jnp.concatenate([mean(s2, s4, 1, 2), mean(s8, s16, 4, 8)], axis=-1) - u_ref[...]


def _token_order(blk_ref, buf, *, tm):
    dilation = blk_ref.shape[0]
    if dilation == 1:
        return blk_ref[0]
    rows = tm // dilation
    n_slabs = GROUP_DIM // LANES
    for r in range(dilation):
        for s in range(n_slabs):
            buf[s, pl.ds(r, rows, stride=dilation), :] = blk_ref[r, :, s * LANES:(s + 1) * LANES]
    return jnp.concatenate([buf[s] for s in range(n_slabs)], axis=-1)


def _mix_out_kernel(x_ref, u_ref, up_ref, un_ref, o1_ref, o2_ref, o3_ref, l1_ref, l2_ref, l3_ref,
                    wpool_ref, pscale_ref, wout_ref, gpost_ref, out_ref, ubuf, abuf, ob2, ob3, lb2, lb3,
                    *, tm, seq):
    y = _pool_mean_minus_token(u_ref, up_ref, un_ref, ubuf, abuf, tm=tm, seq=seq)
    a_pool = jnp.dot(y.astype(BF16), wpool_ref[...], preferred_element_type=F32) * pscale_ref[...]
    l1 = l1_ref[0]
    l2 = _token_order(l2_ref, lb2, tm=tm)
    l3 = _token_order(l3_ref, lb3, tm=tm)
    m = jnp.maximum(jnp.maximum(l1, l2), l3)
    e1, e2, e3 = jnp.exp(l1 - m), jnp.exp(l2 - m), jnp.exp(l3 - m)
    inv_z = 1.0 / (e1 + e2 + e3)
    o2 = _token_order(o2_ref, ob2, tm=tm)
    o3 = _token_order(o3_ref, ob3, tm=tm)
    cat = jnp.concatenate([a_pool, o1_ref[0] * (e1 * inv_z), o2 * (e2 * inv_z), o3 * (e3 * inv_z)],
                          axis=-1).astype(BF16)
    mix = jnp.dot(cat, wout_ref[...], preferred_element_type=F32)
    out_ref[...] = x_ref[...] + _rmsnorm(mix, gpost_ref[...])


def _mix_out(x, u, attn, w_pool_bd, pool_scale, w_out, g_post, *, tm):
    batch, seq, _ = x.shape
    n_halo = seq // POOL_HALO
    halo_per_tm = tm // POOL_HALO
    row = lambda width: pl.BlockSpec((None, tm, width), lambda b, i: (b, i, 0))
    prev = pl.BlockSpec((None, POOL_HALO, POOL_DIM), lambda b, i: (b, jnp.maximum(i * halo_per_tm - 1, 0), 0))
    nxt = pl.BlockSpec((None, POOL_HALO, POOL_DIM),
                       lambda b, i: (b, jnp.minimum((i + 1) * halo_per_tm, n_halo - 1), 0))
    by_residue = [pl.BlockSpec((None, d, tm // d, GROUP_DIM), lambda b, i: (b, 0, i, 0)) for _, d in ATTN_GROUPS]
    (o1, l1), (o2, l2), (o3, l3) = attn
    pool_buf = pltpu.VMEM((tm + 3 * POOL_HALO, POOL_DIM), F32)
    order_buf = pltpu.VMEM((GROUP_DIM // LANES, tm, LANES), F32)
    return pl.pallas_call(
        functools.partial(_mix_out_kernel, tm=tm, seq=seq),
        grid=(batch, seq // tm),
        in_specs=[row(D_MODEL), row(POOL_DIM), prev, nxt] + by_residue + by_residue +
                 [_const_spec((POOL_DIM, POOL_DIM)), _const_spec((1, POOL_DIM)),
                  _const_spec((D_MODEL, D_MODEL)), _const_spec((1, D_MODEL))],
        out_specs=row(D_MODEL),
        out_shape=jax.ShapeDtypeStruct((batch, seq, D_MODEL), F32),
        scratch_shapes=[pool_buf, pool_buf, order_buf, order_buf, order_buf, order_buf],
        compiler_params=_params(2),
        name="mix_out",
    )(x, u, u, u, o1, o2, o3, l1, l2, l3, w_pool_bd, pool_scale, w_out, g_post)


def _ffn_weights(w_gate, w_up, w_down):
    pad = FF_PAD - D_FF
    return (jnp.pad(w_gate.astype(BF16), ((0, 0), (0, pad))),
            jnp.pad(w_up.astype(BF16), ((0, 0), (0, pad))),
            jnp.pad(w_down.astype(BF16), ((0, pad), (0, 0))))


def _block_diag(w_lin):
    g, c, e = w_lin.shape
    eye = jnp.eye(g, dtype=w_lin.dtype)
    return (eye[:, None, :, None] * w_lin[:, :, None, :]).reshape(g * c, g * e)


def _qkv_by_group(w_qkv):
    d_model = w_qkv.shape[0]
    n_groups = len(ATTN_GROUPS)
    return w_qkv.reshape(d_model, 3, n_groups, GROUP_DIM).transpose(0, 2, 1, 3).reshape(d_model, D_QKV)


def kernel(x, g_ffn1_pre, w1_gate, w1_up, w1_down, g_ffn1_post, g_mix_pre, w_in, w_pool_lin, pool_scale,
           w_out, g_mix_post, g_ffn2_pre, w2_gate, w2_up, w2_down, g_ffn2_post):
    batch, seq, _ = x.shape
    depth = g_ffn1_pre.shape[0]
    tm = 512
    assert seq % tm == 0 and tm % (ATTN_GROUPS[-1][1] * 16) == 0
    h = x
    for l in range(depth):
        h = _ffn(h.reshape(batch * seq, D_MODEL), g_ffn1_pre[l][None],
                 *_ffn_weights(w1_gate[l], w1_up[l], w1_down[l]), g_ffn1_post[l][None],
                 tm=2 * tm, ff_chunk=V7X_MXU_DIM).reshape(batch, seq, D_MODEL)
        w_in_bf = w_in[l].astype(BF16)
        u, *qkv = _proj_in(h, g_mix_pre[l][None], w_in_bf[:, :POOL_DIM], _qkv_by_group(w_in_bf[:, POOL_DIM:]), tm=tm)
        attn = [_attention_group(qkv[gi], gi) for gi in range(len(ATTN_GROUPS))]
        h = _mix_out(h, u, attn, _block_diag(w_pool_lin[l]).astype(BF16), pool_scale[l][None],
                     w_out[l].astype(BF16), g_mix_post[l][None], tm=tm)
        h = _ffn(h.reshape(batch * seq, D_MODEL), g_ffn2_pre[l][None],
                 *_ffn_weights(w2_gate[l], w2_up[l], w2_down[l]), g_ffn2_post[l][None],
                 tm=2 * tm, ff_chunk=V7X_MXU_DIM).reshape(batch, seq, D_MODEL)
    return h
```

```python
import functools

import numpy as np
import jax
import jax.numpy as jnp
from jax import lax
from jax.experimental import pallas as pl
from jax.experimental.pallas import tpu as pltpu

D_MODEL = 1024
HEAD_DIM = 64
POOL_WINDOWS = (2, 4, 8, 16)
POOL_GROUP_DIM = 64
POOL_DIM = len(POOL_WINDOWS) * POOL_GROUP_DIM
ATTN_GROUPS = ((128, 1), (512, 4), (2048, 16))
HEADS_PER_GROUP = 4
N_ATTN_HEADS = HEADS_PER_GROUP * len(ATTN_GROUPS)
GROUP_DIM = HEADS_PER_GROUP * HEAD_DIM
ATTN_DIM = N_ATTN_HEADS * HEAD_DIM
D_QKV = 3 * ATTN_DIM
QKV_GROUP_DIM = 3 * GROUP_DIM
D_FF = 2752
RMS_EPS = 1e-6

LANES = 128
V7X_MXU_DIM = 256
V7X_VMEM_LIMIT_BYTES = 56 * 1024 * 1024
FF_PAD = -(-D_FF // V7X_MXU_DIM) * V7X_MXU_DIM
N_SIDE = 64
Q_SUB = 128
K_WIN = Q_SUB + 2 * N_SIDE
ATTN_ROWS_PER_STEP = 1024
POOL_HALO = 8
MASKED = -1e30

F32 = jnp.float32
BF16 = jnp.bfloat16


def _rmsnorm(x, g):
    ms = jnp.mean(x * x, axis=-1, keepdims=True)
    return x * lax.rsqrt(ms + RMS_EPS) * g


def _const_spec(shape):
    return pl.BlockSpec(shape, lambda *_: (0,) * len(shape), pipeline_mode=pl.Buffered(1))


def _params(n_axes):
    return pltpu.CompilerParams(dimension_semantics=("arbitrary",) * n_axes,
                                vmem_limit_bytes=V7X_VMEM_LIMIT_BYTES)


def _ffn_tile(x, gpre_ref, wg_ref, wu_ref, wd_ref, gpost_ref, acc_ref):
    xn = _rmsnorm(x, gpre_ref[...]).astype(BF16)
    for c0 in range(0, FF_PAD, V7X_MXU_DIM):
        c1 = c0 + V7X_MXU_DIM
        gate = jnp.dot(xn, wg_ref[:, c0:c1], preferred_element_type=F32)
        up = jnp.dot(xn, wu_ref[:, c0:c1], preferred_element_type=F32)
        act = (gate * jax.nn.sigmoid(gate) * up).astype(BF16)
        part = jnp.dot(act, wd_ref[c0:c1, :], preferred_element_type=F32)
        if c0 == 0:
            acc_ref[...] = part
        else:
            acc_ref[...] += part
    return x + 0.5 * _rmsnorm(acc_ref[...], gpost_ref[...])


def _ffn_proj_kernel(x_ref, gpre_ref, wg_ref, wu_ref, wd_ref, gpost_ref, gmix_ref, wpin_ref, wqkv_ref,
                     x1_ref, u_ref, q0_ref, q1_ref, q2_ref, acc_ref, zbuf, *, tm):
    x1 = _ffn_tile(x_ref[...], gpre_ref, wg_ref, wu_ref, wd_ref, gpost_ref, acc_ref)
    x1_ref[...] = x1
    h = _rmsnorm(x1, gmix_ref[...]).astype(BF16)
    u_ref[...] = jnp.dot(h, wpin_ref[...], preferred_element_type=F32)
    n_slabs = QKV_GROUP_DIM // LANES
    for gi, (out, (_, dilation)) in enumerate(zip((q0_ref, q1_ref, q2_ref), ATTN_GROUPS)):
        z = jnp.dot(h, wqkv_ref[:, gi * QKV_GROUP_DIM:(gi + 1) * QKV_GROUP_DIM], preferred_element_type=F32)
        if dilation == 1:
            out[0] = z.astype(BF16)
            continue
        for s in range(n_slabs):
            zbuf[s] = z[:, s * LANES:(s + 1) * LANES]
        rows = tm // dilation
        for r in range(dilation):
            for s in range(n_slabs):
                out[r, :, s * LANES:(s + 1) * LANES] = zbuf[s, pl.ds(r, rows, stride=dilation), :].astype(BF16)


def _ffn_proj(x, g_pre, wg, wu, wd, g_post, g_mix, w_pool_in, w_qkv, *, tm):
    batch, seq, _ = x.shape
    row = lambda width: pl.BlockSpec((None, tm, width), lambda b, i: (b, i, 0))
    qkv_specs = [pl.BlockSpec((None, d, tm // d, QKV_GROUP_DIM), lambda b, i: (b, 0, i, 0)) for _, d in ATTN_GROUPS]
    qkv_shapes = [jax.ShapeDtypeStruct((batch, d, seq // d, QKV_GROUP_DIM), BF16) for _, d in ATTN_GROUPS]
    return pl.pallas_call(
        functools.partial(_ffn_proj_kernel, tm=tm),
        grid=(batch, seq // tm),
        in_specs=[row(D_MODEL), _const_spec((1, D_MODEL)), _const_spec((D_MODEL, FF_PAD)),
                  _const_spec((D_MODEL, FF_PAD)), _const_spec((FF_PAD, D_MODEL)), _const_spec((1, D_MODEL)),
                  _const_spec((1, D_MODEL)), _const_spec((D_MODEL, POOL_DIM)), _const_spec((D_MODEL, D_QKV))],
        out_specs=[row(D_MODEL), row(POOL_DIM)] + qkv_specs,
        out_shape=[jax.ShapeDtypeStruct((batch, seq, D_MODEL), F32),
                   jax.ShapeDtypeStruct((batch, seq, POOL_DIM), F32)] + qkv_shapes,
        scratch_shapes=[pltpu.VMEM((tm, D_MODEL), F32), pltpu.VMEM((QKV_GROUP_DIM // LANES, tm, LANES), F32)],
        compiler_params=_params(2),
        name="ffn_proj",
    )(x, g_pre, wg, wu, wd, g_post, g_mix, w_pool_in, w_qkv)


def _attn_kernel(q_ref, kp_ref, kc_ref, kn_ref, vp_ref, vc_ref, vn_ref, o_ref, lse_ref,
                 kwin, vwin, qs_ref, bias_ref, *, dilation, slopes, sub_len, tq, n_res):
    first = (pl.program_id(0) == 0) & (pl.program_id(1) == 0) & (pl.program_id(2) == 0)

    @pl.when(first)
    def _():
        qi = lax.broadcasted_iota(jnp.int32, (Q_SUB, K_WIN), 0)
        kj = lax.broadcasted_iota(jnp.int32, (Q_SUB, K_WIN), 1)
        rel = jnp.abs(kj - N_SIDE - qi)
        dist = (rel * dilation).astype(F32)
        in_band = rel <= N_SIDE
        for variant in range(4):
            ok = in_band
            if variant & 1:
                ok = ok & (kj >= N_SIDE)
            if variant & 2:
                ok = ok & (kj < N_SIDE + Q_SUB)
            for h in range(HEADS_PER_GROUP):
                bias_ref[variant, h * Q_SUB:(h + 1) * Q_SUB, :] = jnp.where(ok, -slopes[h] * dist, MASKED)

    kwin[:, 0:N_SIDE] = kp_ref[...]
    kwin[:, N_SIDE:N_SIDE + tq] = kc_ref[...]
    kwin[:, N_SIDE + tq:] = kn_ref[...]
    vwin[:, 0:N_SIDE] = vp_ref[...]
    vwin[:, N_SIDE:N_SIDE + tq] = vc_ref[...]
    vwin[:, N_SIDE + tq:] = vn_ref[...]

    n_sub = tq // Q_SUB
    sub0 = pl.program_id(2) * n_sub
    last_sub = sub_len // Q_SUB - 1
    lane = lax.broadcasted_iota(jnp.int32, (1, GROUP_DIM), 1)
    head_of_lane = lane >> 6
    scale = HEAD_DIM ** -0.5
    even_head = lax.broadcasted_iota(jnp.int32, (1, LANES), 1) < HEAD_DIM

    for res in range(n_res):
        for j in range(n_sub):
            rows = slice(j * Q_SUB, (j + 1) * Q_SUB)
            q = q_ref[res, rows, :]
            for h in range(HEADS_PER_GROUP):
                qs_ref[h * Q_SUB:(h + 1) * Q_SUB, :] = q * jnp.where(head_of_lane == h, scale, 0.0).astype(BF16)
            kw = kwin[res, j * Q_SUB:j * Q_SUB + K_WIN, :]
            vw = vwin[res, j * Q_SUB:j * Q_SUB + K_WIN, :]
            g = sub0 + j
            variant = (g == 0).astype(jnp.int32) + 2 * (g == last_sub).astype(jnp.int32)
            s = lax.dot_general(qs_ref[...], kw, (((1,), (1,)), ((), ())), preferred_element_type=F32)
            s = s + bias_ref[variant]
            m = jnp.max(s, axis=-1, keepdims=True)
            p = jnp.exp(s - m)
            den = jnp.sum(p, axis=-1, keepdims=True)
            pb = p.astype(BF16)
            p_wide = jnp.concatenate([pb[h * Q_SUB:(h + 1) * Q_SUB] for h in range(HEADS_PER_GROUP)], axis=-1)
            v_tall = jnp.concatenate([vw * (head_of_lane == h).astype(BF16) for h in range(HEADS_PER_GROUP)], axis=0)
            o = jnp.dot(p_wide, v_tall, preferred_element_type=F32)
            lse = m + jnp.log(den)
            for half in range(2):
                cols = slice(half * LANES, (half + 1) * LANES)
                ra = slice(2 * half * Q_SUB, (2 * half + 1) * Q_SUB)
                rb = slice((2 * half + 1) * Q_SUB, (2 * half + 2) * Q_SUB)
                den_half = jnp.where(even_head, den[ra], den[rb])
                o_ref[res, rows, cols] = o[:, cols] / den_half
                lse_ref[res, rows, cols] = jnp.where(even_head, lse[ra], lse[rb])


def _attention_group(qkv, group):
    batch, dilation, sub_len, _ = qkv.shape
    tq = min(ATTN_ROWS_PER_STEP, sub_len)
    n_res = min(ATTN_ROWS_PER_STEP // tq, dilation)
    assert sub_len % tq == 0 and tq % Q_SUB == 0 and dilation % n_res == 0
    halo_per_tq = tq // N_SIDE
    n_halo = sub_len // N_SIDE
    slopes = tuple(float(np.float32(2.0 ** (-8.0 * (group * HEADS_PER_GROUP + h + 1) / N_ATTN_HEADS)))
                   for h in range(HEADS_PER_GROUP))

    def cur(part):
        return pl.BlockSpec((None, n_res, tq, GROUP_DIM), lambda b, r, i: (b, r, i, part))

    def prev(part):
        return pl.BlockSpec((None, n_res, N_SIDE, GROUP_DIM),
                            lambda b, r, i: (b, r, jnp.maximum(i * halo_per_tq - 1, 0), part))

    def nxt(part):
        return pl.BlockSpec((None, n_res, N_SIDE, GROUP_DIM),
                            lambda b, r, i: (b, r, jnp.minimum((i + 1) * halo_per_tq, n_halo - 1), part))

    out_spec = pl.BlockSpec((None, n_res, tq, GROUP_DIM), lambda b, r, i: (b, r, i, 0))
    out_shape = jax.ShapeDtypeStruct((batch, dilation, sub_len, GROUP_DIM), F32)
    return pl.pallas_call(
        functools.partial(_attn_kernel, dilation=dilation, slopes=slopes, sub_len=sub_len, tq=tq, n_res=n_res),
        grid=(batch, dilation // n_res, sub_len // tq),
        in_specs=[cur(0), prev(1), cur(1), nxt(1), prev(2), cur(2), nxt(2)],
        out_specs=[out_spec, out_spec],
        out_shape=[out_shape, out_shape],
        scratch_shapes=[pltpu.VMEM((n_res, tq + 2 * N_SIDE, GROUP_DIM), BF16),
                        pltpu.VMEM((n_res, tq + 2 * N_SIDE, GROUP_DIM), BF16),
                        pltpu.VMEM((HEADS_PER_GROUP * Q_SUB, GROUP_DIM), BF16),
                        pltpu.VMEM((4, HEADS_PER_GROUP * Q_SUB, K_WIN), F32)],
        compiler_params=_params(3),
        name=f"attn_d{dilation}",
    )(qkv, qkv, qkv, qkv, qkv, qkv, qkv)


def _pool_mean_minus_token(u_ref, up_ref, un_ref, ubuf, abuf, *, tm, seq):
    tiles_per_seq = seq // tm
    ti = pl.program_id(1)
    n = tm + 2 * POOL_HALO
    zero_halo = jnp.zeros((POOL_HALO, POOL_DIM), F32)
    ubuf[0:POOL_HALO] = jnp.where(ti == 0, zero_halo, up_ref[...])
    ubuf[POOL_HALO:POOL_HALO + tm] = u_ref[...]
    ubuf[POOL_HALO + tm:n] = jnp.where(ti == tiles_per_seq - 1, zero_halo, un_ref[...])
    ubuf[n:] = zero_halo
    abuf[n:] = zero_halo
    lo, hi = slice(0, LANES), slice(LANES, 2 * LANES)
    a2 = ubuf[0:n, lo] + ubuf[1:n + 1, lo]
    abuf[0:n, lo] = a2
    s2 = abuf[7:7 + tm, lo]
    ubuf[0:n, lo] = a2 + abuf[2:n + 2, lo]
    s4 = ubuf[6:6 + tm, lo]
    a2 = ubuf[0:n, hi] + ubuf[1:n + 1, hi]
    abuf[0:n, hi] = a2
    a4 = a2 + abuf[2:n + 2, hi]
    ubuf[0:n, hi] = a4
    a8 = a4 + ubuf[4:n + 4, hi]
    abuf[0:n, hi] = a8
    s8 = abuf[4:4 + tm, hi]
    s16 = a8[0:tm] + abuf[8:8 + tm, hi]
    first_group = lax.broadcasted_iota(jnp.int32, (1, LANES), 1) < POOL_GROUP_DIM
    t = ti * tm + lax.broadcasted_iota(jnp.int32, (tm, 1), 0)

    def mean(sum_a, sum_b, half_a, half_b):
        half = jnp.where(first_group, half_a, half_b)
        count = jnp.minimum(t + half, seq) - jnp.maximum(t - half, 0)
        return jnp.where(first_group, sum_a, sum_b) / count.astype(F32)

    return jnp.concatenate([mean(s2, s4, 1, 2), mean(s8, s16, 4, 8)], axis=-1) - u_ref[...]


def _token_order(blk_ref, buf, *, tm):
    dilation = blk_ref.shape[0]
    if dilation == 1:
        return blk_ref[0]
    rows = tm // dilation
    n_slabs = GROUP_DIM // LANES
    for r in range(dilation):
        for s in range(n_slabs):
            buf[s, pl.ds(r, rows, stride=dilation), :] = blk_ref[r, :, s * LANES:(s + 1) * LANES]
    return jnp.concatenate([buf[s] for s in range(n_slabs)], axis=-1)


def _mix_ffn_kernel(x_ref, u_ref, up_ref, un_ref, o1_ref, o2_ref, o3_ref, l1_ref, l2_ref, l3_ref,
                    wpool_ref, pscale_ref, wout_ref, gmix_ref, gpre_ref, wg_ref, wu_ref, wd_ref, gpost_ref,
                    out_ref, ubuf, abuf, ob2, ob3, lb2, lb3, acc_ref, *, tm, seq):
    y = _pool_mean_minus_token(u_ref, up_ref, un_ref, ubuf, abuf, tm=tm, seq=seq)
    a_pool = jnp.dot(y.astype(BF16), wpool_ref[...], preferred_element_type=F32) * pscale_ref[...]
    l1 = l1_ref[0]
    l2 = _token_order(l2_ref, lb2, tm=tm)
    l3 = _token_order(l3_ref, lb3, tm=tm)
    m = jnp.maximum(jnp.maximum(l1, l2), l3)
    e1, e2, e3 = jnp.exp(l1 - m), jnp.exp(l2 - m), jnp.exp(l3 - m)
    inv_z = 1.0 / (e1 + e2 + e3)
    o2 = _token_order(o2_ref, ob2, tm=tm)
    o3 = _token_order(o3_ref, ob3, tm=tm)
    cat = jnp.concatenate([a_pool, o1_ref[0] * (e1 * inv_z), o2 * (e2 * inv_z), o3 * (e3 * inv_z)],
                          axis=-1).astype(BF16)
    mix = jnp.dot(cat, wout_ref[...], preferred_element_type=F32)
    x2 = x_ref[...] + _rmsnorm(mix, gmix_ref[...])
    out_ref[...] = _ffn_tile(x2, gpre_ref, wg_ref, wu_ref, wd_ref, gpost_ref, acc_ref)


def _mix_ffn(x, u, attn, w_pool_bd, pool_scale, w_out, g_mix_post, g_pre, wg, wu, wd, g_post, *, tm):
    batch, seq, _ = x.shape
    n_halo = seq // POOL_HALO
    halo_per_tm = tm // POOL_HALO
    row = lambda width: pl.BlockSpec((None, tm, width), lambda b, i: (b, i, 0))
    prev = pl.BlockSpec((None, POOL_HALO, POOL_DIM), lambda b, i: (b, jnp.maximum(i * halo_per_tm - 1, 0), 0))
    nxt = pl.BlockSpec((None, POOL_HALO, POOL_DIM),
                       lambda b, i: (b, jnp.minimum((i + 1) * halo_per_tm, n_halo - 1), 0))
    by_residue = [pl.BlockSpec((None, d, tm // d, GROUP_DIM), lambda b, i: (b, 0, i, 0)) for _, d in ATTN_GROUPS]
    (o1, l1), (o2, l2), (o3, l3) = attn
    pool_buf = pltpu.VMEM((tm + 3 * POOL_HALO, POOL_DIM), F32)
    order_buf = pltpu.VMEM((GROUP_DIM // LANES, tm, LANES), F32)
    return pl.pallas_call(
        functools.partial(_mix_ffn_kernel, tm=tm, seq=seq),
        grid=(batch, seq // tm),
        in_specs=[row(D_MODEL), row(POOL_DIM), prev, nxt] + by_residue + by_residue +
                 [_const_spec((POOL_DIM, POOL_DIM)), _const_spec((1, POOL_DIM)),
                  _const_spec((D_MODEL, D_MODEL)), _const_spec((1, D_MODEL)), _const_spec((1, D_MODEL)),
                  _const_spec((D_MODEL, FF_PAD)), _const_spec((D_MODEL, FF_PAD)), _const_spec((FF_PAD, D_MODEL)),
                  _const_spec((1, D_MODEL))],
        out_specs=row(D_MODEL),
        out_shape=jax.ShapeDtypeStruct((batch, seq, D_MODEL), F32),
        scratch_shapes=[pool_buf, pool_buf, order_buf, order_buf, order_buf, order_buf,
                        pltpu.VMEM((tm, D_MODEL), F32)],
        compiler_params=_params(2),
        name="mix_ffn",
    )(x, u, u, u, o1, o2, o3, l1, l2, l3, w_pool_bd, pool_scale, w_out, g_mix_post, g_pre, wg, wu, wd, g_post)


def _ffn_weights(w_gate, w_up, w_down):
    pad = FF_PAD - D_FF
    return (jnp.pad(w_gate, ((0, 0), (0, pad))).astype(BF16),
            jnp.pad(w_up, ((0, 0), (0, pad))).astype(BF16),
            jnp.pad(w_down, ((0, pad), (0, 0))).astype(BF16))


def _block_diag(w_lin):
    g, c, e = w_lin.shape
    eye = jnp.eye(g, dtype=w_lin.dtype)
    return (eye[:, None, :, None] * w_lin[:, :, None, :]).reshape(g * c, g * e)


def _qkv_by_group(w_qkv):
    d_model = w_qkv.shape[0]
    n_groups = len(ATTN_GROUPS)
    return w_qkv.reshape(d_model, 3, n_groups, GROUP_DIM).transpose(0, 2, 1, 3).reshape(d_model, D_QKV)


def kernel(x, g_ffn1_pre, w1_gate, w1_up, w1_down, g_ffn1_post, g_mix_pre, w_in, w_pool_lin, pool_scale,
           w_out, g_mix_post, g_ffn2_pre, w2_gate, w2_up, w2_down, g_ffn2_post):
    batch, seq, _ = x.shape
    depth = g_ffn1_pre.shape[0]
    tm = 512
    assert seq % tm == 0 and tm % (ATTN_GROUPS[-1][1] * 16) == 0
    h = x
    for l in range(depth):
        w_in_bf = w_in[l].astype(BF16)
        x1, u, *qkv = _ffn_proj(h, g_ffn1_pre[l][None], *_ffn_weights(w1_gate[l], w1_up[l], w1_down[l]),
                                g_ffn1_post[l][None], g_mix_pre[l][None], w_in_bf[:, :POOL_DIM],
                                _qkv_by_group(w_in_bf[:, POOL_DIM:]), tm=tm)
        attn = [_attention_group(qkv[gi], gi) for gi in range(len(ATTN_GROUPS))]
        h = _mix_ffn(x1, u, attn, _block_diag(w_pool_lin[l]).astype(BF16), pool_scale[l][None],
                     w_out[l].astype(BF16), g_mix_post[l][None], g_ffn2_pre[l][None],
                     *_ffn_weights(w2_gate[l], w2_up[l], w2_down[l]), g_ffn2_post[l][None], tm=tm)
    return h
```

```python
import functools

import numpy as np
import jax
import jax.numpy as jnp
from jax import lax
from jax.experimental import pallas as pl
from jax.experimental.pallas import tpu as pltpu

D_MODEL = 1024
HEAD_DIM = 64
POOL_WINDOWS = (2, 4, 8, 16)
POOL_GROUP_DIM = 64
POOL_DIM = len(POOL_WINDOWS) * POOL_GROUP_DIM
ATTN_GROUPS = ((128, 1), (512, 4), (2048, 16))
HEADS_PER_GROUP = 4
N_ATTN_HEADS = HEADS_PER_GROUP * len(ATTN_GROUPS)
GROUP_DIM = HEADS_PER_GROUP * HEAD_DIM
ATTN_DIM = N_ATTN_HEADS * HEAD_DIM
D_QKV = 3 * ATTN_DIM
QKV_GROUP_DIM = 3 * GROUP_DIM
D_FF = 2752
RMS_EPS = 1e-6

LANES = 128
V7X_MXU_DIM = 256
V7X_VMEM_LIMIT_BYTES = 56 * 1024 * 1024
N_SIDE = 64
Q_SUB = 128
K_WIN = Q_SUB + 2 * N_SIDE
ATTN_ROWS_PER_STEP = 1024
SPLIT_STRIDE = 4
POOL_HALO = 8
MASKED = -1e30

F32 = jnp.float32
BF16 = jnp.bfloat16


def _rmsnorm(x, g):
    ms = jnp.mean(x * x, axis=-1, keepdims=True)
    return x * lax.rsqrt(ms + RMS_EPS) * g


def _const_spec(shape):
    return pl.BlockSpec(shape, lambda *_: (0,) * len(shape), pipeline_mode=pl.Buffered(1))


def _params(n_axes):
    return pltpu.CompilerParams(dimension_semantics=("arbitrary",) * n_axes,
                                vmem_limit_bytes=V7X_VMEM_LIMIT_BYTES)


def _ffn_tile(x, gpre_ref, wg_ref, wu_ref, wd_ref, gpost_ref, acc_ref):
    xn = _rmsnorm(x, gpre_ref[...]).astype(BF16)
    for c0 in range(0, D_FF, V7X_MXU_DIM):
        c1 = min(c0 + V7X_MXU_DIM, D_FF)
        gate = jnp.dot(xn, wg_ref[:, c0:c1], preferred_element_type=F32)
        up = jnp.dot(xn, wu_ref[:, c0:c1], preferred_element_type=F32)
        act = (gate * jax.nn.sigmoid(gate) * up).astype(BF16)
        part = jnp.dot(act, wd_ref[c0:c1, :], preferred_element_type=F32)
        if c0 == 0:
            acc_ref[...] = part
        else:
            acc_ref[...] += part
    return x + 0.5 * _rmsnorm(acc_ref[...], gpost_ref[...])


def _ffn_proj_kernel(x_ref, gpre_ref, wg_ref, wu_ref, wd_ref, gpost_ref, gmix_ref, wpin_ref, wqkv_ref,
                     x1_ref, u_ref, q0_ref, q1_ref, q2_ref, acc_ref, zbuf, *, tm):
    x1 = _ffn_tile(x_ref[...], gpre_ref, wg_ref, wu_ref, wd_ref, gpost_ref, acc_ref)
    x1_ref[...] = x1
    h = _rmsnorm(x1, gmix_ref[...]).astype(BF16)
    u_ref[...] = jnp.dot(h, wpin_ref[...], preferred_element_type=F32)
    n_slabs = QKV_GROUP_DIM // LANES
    for gi, (out, (_, dilation)) in enumerate(zip((q0_ref, q1_ref, q2_ref), ATTN_GROUPS)):
        z = jnp.dot(h, wqkv_ref[:, gi * QKV_GROUP_DIM:(gi + 1) * QKV_GROUP_DIM], preferred_element_type=F32)
        if dilation == 1:
            out[0] = z.astype(BF16)
            continue
        for s in range(n_slabs):
            zbuf[0, s] = z[:, s * LANES:(s + 1) * LANES]
        part = tm // SPLIT_STRIDE
        if dilation == SPLIT_STRIDE:
            for r in range(dilation):
                for s in range(n_slabs):
                    out[r, :, s * LANES:(s + 1) * LANES] = (
                        zbuf[0, s, pl.ds(r, part, stride=SPLIT_STRIDE), :].astype(BF16))
            continue
        assert dilation == SPLIT_STRIDE * SPLIT_STRIDE
        for r_lo in range(SPLIT_STRIDE):
            for s in range(n_slabs):
                zbuf[1, s, r_lo * part:(r_lo + 1) * part, :] = zbuf[0, s, pl.ds(r_lo, part, stride=SPLIT_STRIDE), :]
        for r_lo in range(SPLIT_STRIDE):
            for r_hi in range(SPLIT_STRIDE):
                for s in range(n_slabs):
                    out[r_lo + SPLIT_STRIDE * r_hi, :, s * LANES:(s + 1) * LANES] = (
                        zbuf[1, s, pl.ds(r_lo * part + r_hi, tm // dilation, stride=SPLIT_STRIDE), :].astype(BF16))


def _ffn_proj(x, g_pre, wg, wu, wd, g_post, g_mix, w_pool_in, w_qkv, *, tm):
    batch, seq, _ = x.shape
    row = lambda width: pl.BlockSpec((None, tm, width), lambda b, i: (b, i, 0))
    qkv_specs = [pl.BlockSpec((None, d, tm // d, QKV_GROUP_DIM), lambda b, i: (b, 0, i, 0)) for _, d in ATTN_GROUPS]
    qkv_shapes = [jax.ShapeDtypeStruct((batch, d, seq // d, QKV_GROUP_DIM), BF16) for _, d in ATTN_GROUPS]
    return pl.pallas_call(
        functools.partial(_ffn_proj_kernel, tm=tm),
        grid=(batch, seq // tm),
        in_specs=[row(D_MODEL), _const_spec((1, D_MODEL)), _const_spec((D_MODEL, D_FF)),
                  _const_spec((D_MODEL, D_FF)), _const_spec((D_FF, D_MODEL)), _const_spec((1, D_MODEL)),
                  _const_spec((1, D_MODEL)), _const_spec((D_MODEL, POOL_DIM)), _const_spec((D_MODEL, D_QKV))],
        out_specs=[row(D_MODEL), row(POOL_DIM)] + qkv_specs,
        out_shape=[jax.ShapeDtypeStruct((batch, seq, D_MODEL), F32),
                   jax.ShapeDtypeStruct((batch, seq, POOL_DIM), F32)] + qkv_shapes,
        scratch_shapes=[pltpu.VMEM((tm, D_MODEL), F32),
                        pltpu.VMEM((len(ATTN_GROUPS) - 1, QKV_GROUP_DIM // LANES, tm, LANES), F32)],
        compiler_params=_params(2),
        name="ffn_proj",
    )(x, g_pre, wg, wu, wd, g_post, g_mix, w_pool_in, w_qkv)


def _attn_kernel(q_ref, kp_ref, kc_ref, kn_ref, vp_ref, vc_ref, vn_ref, o_ref, lse_ref,
                 kwin, vwin, qs_ref, bias_ref, *, dilation, slopes, sub_len, tq, n_res):
    first = (pl.program_id(0) == 0) & (pl.program_id(1) == 0) & (pl.program_id(2) == 0)

    @pl.when(first)
    def _():
        qi = lax.broadcasted_iota(jnp.int32, (Q_SUB, K_WIN), 0)
        kj = lax.broadcasted_iota(jnp.int32, (Q_SUB, K_WIN), 1)
        rel = jnp.abs(kj - N_SIDE - qi)
        dist = (rel * dilation).astype(F32)
        in_band = rel <= N_SIDE
        for variant in range(4):
            ok = in_band
            if variant & 1:
                ok = ok & (kj >= N_SIDE)
            if variant & 2:
                ok = ok & (kj < N_SIDE + Q_SUB)
            for h in range(HEADS_PER_GROUP):
                bias_ref[variant, h * Q_SUB:(h + 1) * Q_SUB, :] = jnp.where(ok, -slopes[h] * dist, MASKED)

    kwin[:, 0:N_SIDE] = kp_ref[...]
    kwin[:, N_SIDE:N_SIDE + tq] = kc_ref[...]
    kwin[:, N_SIDE + tq:] = kn_ref[...]
    vwin[:, 0:N_SIDE] = vp_ref[...]
    vwin[:, N_SIDE:N_SIDE + tq] = vc_ref[...]
    vwin[:, N_SIDE + tq:] = vn_ref[...]

    n_sub = tq // Q_SUB
    sub0 = pl.program_id(2) * n_sub
    last_sub = sub_len // Q_SUB - 1
    lane = lax.broadcasted_iota(jnp.int32, (1, GROUP_DIM), 1)
    head_of_lane = lane >> 6
    scale = HEAD_DIM ** -0.5
    even_head = lax.broadcasted_iota(jnp.int32, (1, LANES), 1) < HEAD_DIM

    for res in range(n_res):
        for j in range(n_sub):
            rows = slice(j * Q_SUB, (j + 1) * Q_SUB)
            q = q_ref[res, rows, :]
            for h in range(HEADS_PER_GROUP):
                qs_ref[h * Q_SUB:(h + 1) * Q_SUB, :] = q * jnp.where(head_of_lane == h, scale, 0.0).astype(BF16)
            kw = kwin[res, j * Q_SUB:j * Q_SUB + K_WIN, :]
            vw = vwin[res, j * Q_SUB:j * Q_SUB + K_WIN, :]
            g = sub0 + j
            variant = (g == 0).astype(jnp.int32) + 2 * (g == last_sub).astype(jnp.int32)
            s = lax.dot_general(qs_ref[...], kw, (((1,), (1,)), ((), ())), preferred_element_type=F32)
            s = s + bias_ref[variant]
            m = jnp.max(s, axis=-1, keepdims=True)
            p = jnp.exp(s - m)
            den = jnp.sum(p, axis=-1, keepdims=True)
            pb = p.astype(BF16)
            p_wide = jnp.concatenate([pb[h * Q_SUB:(h + 1) * Q_SUB] for h in range(HEADS_PER_GROUP)], axis=-1)
            v_tall = jnp.concatenate([vw * (head_of_lane == h).astype(BF16) for h in range(HEADS_PER_GROUP)], axis=0)
            o = jnp.dot(p_wide, v_tall, preferred_element_type=F32)
            lse = m + jnp.log(den)
            for half in range(2):
                cols = slice(half * LANES, (half + 1) * LANES)
                ra = slice(2 * half * Q_SUB, (2 * half + 1) * Q_SUB)
                rb = slice((2 * half + 1) * Q_SUB, (2 * half + 2) * Q_SUB)
                den_half = jnp.where(even_head, den[ra], den[rb])
                o_ref[res, rows, cols] = o[:, cols] / den_half
                lse_ref[res, rows, cols] = jnp.where(even_head, lse[ra], lse[rb])


def _attention_group(qkv, group):
    batch, dilation, sub_len, _ = qkv.shape
    tq = min(ATTN_ROWS_PER_STEP, sub_len)
    n_res = min(ATTN_ROWS_PER_STEP // tq, dilation)
    assert sub_len % tq == 0 and tq % Q_SUB == 0 and dilation % n_res == 0
    halo_per_tq = tq // N_SIDE
    n_halo = sub_len // N_SIDE
    slopes = tuple(float(np.float32(2.0 ** (-8.0 * (group * HEADS_PER_GROUP + h + 1) / N_ATTN_HEADS)))
                   for h in range(HEADS_PER_GROUP))

    def cur(part):
        return pl.BlockSpec((None, n_res, tq, GROUP_DIM), lambda b, r, i: (b, r, i, part))

    def prev(part):
        return pl.BlockSpec((None, n_res, N_SIDE, GROUP_DIM),
                            lambda b, r, i: (b, r, jnp.maximum(i * halo_per_tq - 1, 0), part))

    def nxt(part):
        return pl.BlockSpec((None, n_res, N_SIDE, GROUP_DIM),
                            lambda b, r, i: (b, r, jnp.minimum((i + 1) * halo_per_tq, n_halo - 1), part))

    out_spec = pl.BlockSpec((None, n_res, tq, GROUP_DIM), lambda b, r, i: (b, r, i, 0))
    out_shape = jax.ShapeDtypeStruct((batch, dilation, sub_len, GROUP_DIM), F32)
    return pl.pallas_call(
        functools.partial(_attn_kernel, dilation=dilation, slopes=slopes, sub_len=sub_len, tq=tq, n_res=n_res),
        grid=(batch, dilation // n_res, sub_len // tq),
        in_specs=[cur(0), prev(1), cur(1), nxt(1), prev(2), cur(2), nxt(2)],
        out_specs=[out_spec, out_spec],
        out_shape=[out_shape, out_shape],
        scratch_shapes=[pltpu.VMEM((n_res, tq + 2 * N_SIDE, GROUP_DIM), BF16),
                        pltpu.VMEM((n_res, tq + 2 * N_SIDE, GROUP_DIM), BF16),
                        pltpu.VMEM((HEADS_PER_GROUP * Q_SUB, GROUP_DIM), BF16),
                        pltpu.VMEM((4, HEADS_PER_GROUP * Q_SUB, K_WIN), F32)],
        compiler_params=_params(3),
        name=f"attn_d{dilation}",
    )(qkv, qkv, qkv, qkv, qkv, qkv, qkv)


def _pool_mean_minus_token(u_ref, up_ref, un_ref, ubuf, abuf, *, tm, seq):
    tiles_per_seq = seq // tm
    ti = pl.program_id(1)
    n = tm + 2 * POOL_HALO
    zero_halo = jnp.zeros((POOL_HALO, POOL_DIM), F32)
    ubuf[0:POOL_HALO] = jnp.where(ti == 0, zero_halo, up_ref[...])
    ubuf[POOL_HALO:POOL_HALO + tm] = u_ref[...]
    ubuf[POOL_HALO + tm:n] = jnp.where(ti == tiles_per_seq - 1, zero_halo, un_ref[...])
    ubuf[n:] = zero_halo
    abuf[n:] = zero_halo
    lo, hi = slice(0, LANES), slice(LANES, 2 * LANES)
    a2 = ubuf[0:n, lo] + ubuf[1:n + 1, lo]
    abuf[0:n, lo] = a2
    s2 = abuf[7:7 + tm, lo]
    ubuf[0:n, lo] = a2 + abuf[2:n + 2, lo]
    s4 = ubuf[6:6 + tm, lo]
    a2 = ubuf[0:n, hi] + ubuf[1:n + 1, hi]
    abuf[0:n, hi] = a2
    a4 = a2 + abuf[2:n + 2, hi]
    ubuf[0:n, hi] = a4
    a8 = a4 + ubuf[4:n + 4, hi]
    abuf[0:n, hi] = a8
    s8 = abuf[4:4 + tm, hi]
    s16 = a8[0:tm] + abuf[8:8 + tm, hi]
    first_group = lax.broadcasted_iota(jnp.int32, (1, LANES), 1) < POOL_GROUP_DIM
    t = ti * tm + lax.broadcasted_iota(jnp.int32, (tm, 1), 0)

    def mean(sum_a, sum_b, half_a, half_b):
        half = jnp.where(first_group, half_a, half_b)
        count = jnp.minimum(t + half, seq) - jnp.maximum(t - half, 0)
        return jnp.where(first_group, sum_a, sum_b) / count.astype(F32)

    return jnp.concatenate([mean(s2, s4, 1, 2), mean(s8, s16, 4, 8)], axis=-1) - u_ref[...]


def _token_order(blk_ref, buf, *, tm):
    dilation = blk_ref.shape[0]
    if dilation == 1:
        return blk_ref[0]
    n_slabs = GROUP_DIM // LANES
    part = tm // SPLIT_STRIDE
    if dilation == SPLIT_STRIDE:
        for r in range(dilation):
            for s in range(n_slabs):
                buf[0, s, pl.ds(r, part, stride=SPLIT_STRIDE), :] = blk_ref[r, :, s * LANES:(s + 1) * LANES]
        return jnp.concatenate([buf[0, s] for s in range(n_slabs)], axis=-1)
    assert dilation == SPLIT_STRIDE * SPLIT_STRIDE
    for r_lo in range(SPLIT_STRIDE):
        for r_hi in range(SPLIT_STRIDE):
            for s in range(n_slabs):
                buf[1, s, pl.ds(r_lo * part + r_hi, tm // dilation, stride=SPLIT_STRIDE), :] = (
                    blk_ref[r_lo + SPLIT_STRIDE * r_hi, :, s * LANES:(s + 1) * LANES])
    for r_lo in range(SPLIT_STRIDE):
        for s in range(n_slabs):
            buf[0, s, pl.ds(r_lo, part, stride=SPLIT_STRIDE), :] = buf[1, s, r_lo * part:(r_lo + 1) * part, :]
    return jnp.concatenate([buf[0, s] for s in range(n_slabs)], axis=-1)


def _mix_ffn_kernel(x_ref, u_ref, up_ref, un_ref, o1_ref, o2_ref, o3_ref, l1_ref, l2_ref, l3_ref,
                    wpool_ref, pscale_ref, wout_ref, gmix_ref, gpre_ref, wg_ref, wu_ref, wd_ref, gpost_ref,
                    out_ref, ubuf, abuf, ob2, ob3, lb2, lb3, acc_ref, *, tm, seq):
    y = _pool_mean_minus_token(u_ref, up_ref, un_ref, ubuf, abuf, tm=tm, seq=seq)
    a_pool = jnp.dot(y.astype(BF16), wpool_ref[...], preferred_element_type=F32) * pscale_ref[...]
    l1 = l1_ref[0]
    l2 = _token_order(l2_ref, lb2, tm=tm)
    l3 = _token_order(l3_ref, lb3, tm=tm)
    m = jnp.maximum(jnp.maximum(l1, l2), l3)
    e1, e2, e3 = jnp.exp(l1 - m), jnp.exp(l2 - m), jnp.exp(l3 - m)
    inv_z = 1.0 / (e1 + e2 + e3)
    o2 = _token_order(o2_ref, ob2, tm=tm)
    o3 = _token_order(o3_ref, ob3, tm=tm)
    cat = jnp.concatenate([a_pool, o1_ref[0] * (e1 * inv_z), o2 * (e2 * inv_z), o3 * (e3 * inv_z)],
                          axis=-1).astype(BF16)
    mix = jnp.dot(cat, wout_ref[...], preferred_element_type=F32)
    x2 = x_ref[...] + _rmsnorm(mix, gmix_ref[...])
    out_ref[...] = _ffn_tile(x2, gpre_ref, wg_ref, wu_ref, wd_ref, gpost_ref, acc_ref)


def _mix_ffn(x, u, attn, w_pool_bd, pool_scale, w_out, g_mix_post, g_pre, wg, wu, wd, g_post, *, tm):
    batch, seq, _ = x.shape
    n_halo = seq // POOL_HALO
    halo_per_tm = tm // POOL_HALO
    row = lambda width: pl.BlockSpec((None, tm, width), lambda b, i: (b, i, 0))
    prev = pl.BlockSpec((None, POOL_HALO, POOL_DIM), lambda b, i: (b, jnp.maximum(i * halo_per_tm - 1, 0), 0))
    nxt = pl.BlockSpec((None, POOL_HALO, POOL_DIM),
                       lambda b, i: (b, jnp.minimum((i + 1) * halo_per_tm, n_halo - 1), 0))
    by_residue = [pl.BlockSpec((None, d, tm // d, GROUP_DIM), lambda b, i: (b, 0, i, 0)) for _, d in ATTN_GROUPS]
    (o1, l1), (o2, l2), (o3, l3) = attn
    pool_buf = pltpu.VMEM((tm + 3 * POOL_HALO, POOL_DIM), F32)
    order_buf = pltpu.VMEM((2, GROUP_DIM // LANES, tm, LANES), F32)
    return pl.pallas_call(
        functools.partial(_mix_ffn_kernel, tm=tm, seq=seq),
        grid=(batch, seq // tm),
        in_specs=[row(D_MODEL), row(POOL_DIM), prev, nxt] + by_residue + by_residue +
                 [_const_spec((POOL_DIM, POOL_DIM)), _const_spec((1, POOL_DIM)),
                  _const_spec((D_MODEL, D_MODEL)), _const_spec((1, D_MODEL)), _const_spec((1, D_MODEL)),
                  _const_spec((D_MODEL, D_FF)), _const_spec((D_MODEL, D_FF)), _const_spec((D_FF, D_MODEL)),
                  _const_spec((1, D_MODEL))],
        out_specs=row(D_MODEL),
        out_shape=jax.ShapeDtypeStruct((batch, seq, D_MODEL), F32),
        scratch_shapes=[pool_buf, pool_buf, order_buf, order_buf, order_buf, order_buf,
                        pltpu.VMEM((tm, D_MODEL), F32)],
        compiler_params=_params(2),
        name="mix_ffn",
    )(x, u, u, u, o1, o2, o3, l1, l2, l3, w_pool_bd, pool_scale, w_out, g_mix_post, g_pre, wg, wu, wd, g_post)


def _ffn_weights(w_gate, w_up, w_down):
    return w_gate.astype(BF16), w_up.astype(BF16), w_down.astype(BF16)


def _block_diag(w_lin):
    g, c, e = w_lin.shape
    eye = jnp.eye(g, dtype=w_lin.dtype)
    return (eye[:, None, :, None] * w_lin[:, :, None, :]).reshape(g * c, g * e)


def _qkv_by_group(w_qkv):
    d_model = w_qkv.shape[0]
    n_groups = len(ATTN_GROUPS)
    return w_qkv.reshape(d_model, 3, n_groups, GROUP_DIM).transpose(0, 2, 1, 3).reshape(d_model, D_QKV)


def kernel(x, g_ffn1_pre, w1_gate, w1_up, w1_down, g_ffn1_post, g_mix_pre, w_in, w_pool_lin, pool_scale,
           w_out, g_mix_post, g_ffn2_pre, w2_gate, w2_up, w2_down, g_ffn2_post):
    batch, seq, _ = x.shape
    depth = g_ffn1_pre.shape[0]
    tm = 512
    assert seq % tm == 0 and tm % (ATTN_GROUPS[-1][1] * 16) == 0
    h = x
    for l in range(depth):
        w_in_bf = w_in[l].astype(BF16)
        x1, u, *qkv = _ffn_proj(h, g_ffn1_pre[l][None], *_ffn_weights(w1_gate[l], w1_up[l], w1_down[l]),
                                g_ffn1_post[l][None], g_mix_pre[l][None], w_in_bf[:, :POOL_DIM],
                                _qkv_by_group(w_in_bf[:, POOL_DIM:]), tm=tm)
        attn = [_attention_group(qkv[gi], gi) for gi in range(len(ATTN_GROUPS))]
        h = _mix_ffn(x1, u, attn, _block_diag(w_pool_lin[l]).astype(BF16), pool_scale[l][None],
                     w_out[l].astype(BF16), g_mix_post[l][None], g_ffn2_pre[l][None],
                     *_ffn_weights(w2_gate[l], w2_up[l], w2_down[l]), g_ffn2_post[l][None], tm=tm)
    return h
```

```python
import functools

import numpy as np
import jax
import jax.numpy as jnp
from jax import lax
from jax.experimental import pallas as pl
from jax.experimental.pallas import tpu as pltpu

D_MODEL = 1024
HEAD_DIM = 64
POOL_WINDOWS = (2, 4, 8, 16)
POOL_GROUP_DIM = 64
POOL_DIM = len(POOL_WINDOWS) * POOL_GROUP_DIM
ATTN_GROUPS = ((128, 1), (512, 4), (2048, 16))
HEADS_PER_GROUP = 4
N_ATTN_HEADS = HEADS_PER_GROUP * len(ATTN_GROUPS)
GROUP_DIM = HEADS_PER_GROUP * HEAD_DIM
ATTN_DIM = N_ATTN_HEADS * HEAD_DIM
D_QKV = 3 * ATTN_DIM
QKV_GROUP_DIM = 3 * GROUP_DIM
D_FF = 2752
RMS_EPS = 1e-6

LANES = 128
V7X_MXU_DIM = 256
V7X_VMEM_LIMIT_BYTES = 56 * 1024 * 1024
N_SIDE = 64
Q_SUB = 128
K_WIN = Q_SUB + 2 * N_SIDE
ATTN_ROWS_PER_STEP = 1024
SPLIT_STRIDE = 4
WEIGHT_CAST_STEPS = 4
POOL_HALO = 8
MASKED = -1e30

F32 = jnp.float32
BF16 = jnp.bfloat16


def _rmsnorm(x, g):
    ms = jnp.mean(x * x, axis=-1, keepdims=True)
    return x * lax.rsqrt(ms + RMS_EPS) * g


def _const_spec(shape):
    return pl.BlockSpec(shape, lambda *_: (0,) * len(shape), pipeline_mode=pl.Buffered(1))


def _params(n_axes):
    return pltpu.CompilerParams(dimension_semantics=("arbitrary",) * n_axes,
                                vmem_limit_bytes=V7X_VMEM_LIMIT_BYTES)


def _ffn_tile(x, gpre_ref, wg_ref, wu_ref, wd_ref, gpost_ref, acc_ref):
    xn = _rmsnorm(x, gpre_ref[...]).astype(BF16)
    for c0 in range(0, D_FF, V7X_MXU_DIM):
        c1 = min(c0 + V7X_MXU_DIM, D_FF)
        gate = jnp.dot(xn, wg_ref[:, c0:c1], preferred_element_type=F32)
        up = jnp.dot(xn, wu_ref[:, c0:c1], preferred_element_type=F32)
        act = (gate * jax.nn.sigmoid(gate) * up).astype(BF16)
        part = jnp.dot(act, wd_ref[c0:c1, :], preferred_element_type=F32)
        if c0 == 0:
            acc_ref[...] = part
        else:
            acc_ref[...] += part
    return x + 0.5 * _rmsnorm(acc_ref[...], gpost_ref[...])


def _ffn_proj_kernel(x_ref, gpre_ref, wg_ref, wu_ref, wd_ref, gpost_ref, gmix_ref, wpin_ref, wqkv_ref,
                     x1_ref, u_ref, q0_ref, q1_ref, q2_ref, acc_ref, zbuf, *, tm):
    x1 = _ffn_tile(x_ref[...], gpre_ref, wg_ref, wu_ref, wd_ref, gpost_ref, acc_ref)
    x1_ref[...] = x1
    h = _rmsnorm(x1, gmix_ref[...]).astype(BF16)
    u_ref[...] = jnp.dot(h, wpin_ref[...], preferred_element_type=F32)
    n_slabs = QKV_GROUP_DIM // LANES
    for gi, (out, (_, dilation)) in enumerate(zip((q0_ref, q1_ref, q2_ref), ATTN_GROUPS)):
        z = jnp.dot(h, wqkv_ref[:, gi * QKV_GROUP_DIM:(gi + 1) * QKV_GROUP_DIM], preferred_element_type=F32)
        if dilation == 1:
            out[0] = z.astype(BF16)
            continue
        for s in range(n_slabs):
            zbuf[0, s] = z[:, s * LANES:(s + 1) * LANES]
        part = tm // SPLIT_STRIDE
        if dilation == SPLIT_STRIDE:
            for r in range(dilation):
                for s in range(n_slabs):
                    out[r, :, s * LANES:(s + 1) * LANES] = (
                        zbuf[0, s, pl.ds(r, part, stride=SPLIT_STRIDE), :].astype(BF16))
            continue
        assert dilation == SPLIT_STRIDE * SPLIT_STRIDE
        for r_lo in range(SPLIT_STRIDE):
            for s in range(n_slabs):
                zbuf[1, s, r_lo * part:(r_lo + 1) * part, :] = zbuf[0, s, pl.ds(r_lo, part, stride=SPLIT_STRIDE), :]
        for r_lo in range(SPLIT_STRIDE):
            for r_hi in range(SPLIT_STRIDE):
                for s in range(n_slabs):
                    out[r_lo + SPLIT_STRIDE * r_hi, :, s * LANES:(s + 1) * LANES] = (
                        zbuf[1, s, pl.ds(r_lo * part + r_hi, tm // dilation, stride=SPLIT_STRIDE), :].astype(BF16))


def _ffn_proj(x, g_pre, wg, wu, wd, g_post, g_mix, w_pool_in, w_qkv, *, tm):
    batch, seq, _ = x.shape
    row = lambda width: pl.BlockSpec((None, tm, width), lambda b, i: (b, i, 0))
    qkv_specs = [pl.BlockSpec((None, d, tm // d, QKV_GROUP_DIM), lambda b, i: (b, 0, i, 0)) for _, d in ATTN_GROUPS]
    qkv_shapes = [jax.ShapeDtypeStruct((batch, d, seq // d, QKV_GROUP_DIM), BF16) for _, d in ATTN_GROUPS]
    return pl.pallas_call(
        functools.partial(_ffn_proj_kernel, tm=tm),
        grid=(batch, seq // tm),
        in_specs=[row(D_MODEL), _const_spec((1, D_MODEL)), _const_spec((D_MODEL, D_FF)),
                  _const_spec((D_MODEL, D_FF)), _const_spec((D_FF, D_MODEL)), _const_spec((1, D_MODEL)),
                  _const_spec((1, D_MODEL)), _const_spec((D_MODEL, POOL_DIM)), _const_spec((D_MODEL, D_QKV))],
        out_specs=[row(D_MODEL), row(POOL_DIM)] + qkv_specs,
        out_shape=[jax.ShapeDtypeStruct((batch, seq, D_MODEL), F32),
                   jax.ShapeDtypeStruct((batch, seq, POOL_DIM), F32)] + qkv_shapes,
        scratch_shapes=[pltpu.VMEM((tm, D_MODEL), F32),
                        pltpu.VMEM((len(ATTN_GROUPS) - 1, QKV_GROUP_DIM // LANES, tm, LANES), F32)],
        compiler_params=_params(2),
        name="ffn_proj",
    )(x, g_pre, wg, wu, wd, g_post, g_mix, w_pool_in, w_qkv)


def _attn_kernel(q_ref, kp_ref, kc_ref, kn_ref, vp_ref, vc_ref, vn_ref, o_ref, lse_ref,
                 kwin, vwin, qs_ref, bias_ref, *, dilation, slopes, sub_len, tq, n_res):
    first = (pl.program_id(0) == 0) & (pl.program_id(1) == 0) & (pl.program_id(2) == 0)

    @pl.when(first)
    def _():
        qi = lax.broadcasted_iota(jnp.int32, (Q_SUB, K_WIN), 0)
        kj = lax.broadcasted_iota(jnp.int32, (Q_SUB, K_WIN), 1)
        rel = jnp.abs(kj - N_SIDE - qi)
        dist = (rel * dilation).astype(F32)
        in_band = rel <= N_SIDE
        for variant in range(4):
            ok = in_band
            if variant & 1:
                ok = ok & (kj >= N_SIDE)
            if variant & 2:
                ok = ok & (kj < N_SIDE + Q_SUB)
            for h in range(HEADS_PER_GROUP):
                bias_ref[variant, h * Q_SUB:(h + 1) * Q_SUB, :] = jnp.where(ok, -slopes[h] * dist, MASKED)

    kwin[:, 0:N_SIDE] = kp_ref[...]
    kwin[:, N_SIDE:N_SIDE + tq] = kc_ref[...]
    kwin[:, N_SIDE + tq:] = kn_ref[...]
    vwin[:, 0:N_SIDE] = vp_ref[...]
    vwin[:, N_SIDE:N_SIDE + tq] = vc_ref[...]
    vwin[:, N_SIDE + tq:] = vn_ref[...]

    n_sub = tq // Q_SUB
    sub0 = pl.program_id(2) * n_sub
    last_sub = sub_len // Q_SUB - 1
    lane = lax.broadcasted_iota(jnp.int32, (1, GROUP_DIM), 1)
    head_of_lane = lane >> 6
    scale = HEAD_DIM ** -0.5
    even_head = lax.broadcasted_iota(jnp.int32, (1, LANES), 1) < HEAD_DIM

    for res in range(n_res):
        for j in range(n_sub):
            rows = slice(j * Q_SUB, (j + 1) * Q_SUB)
            q = q_ref[res, rows, :]
            for h in range(HEADS_PER_GROUP):
                qs_ref[h * Q_SUB:(h + 1) * Q_SUB, :] = q * jnp.where(head_of_lane == h, scale, 0.0).astype(BF16)
            kw = kwin[res, j * Q_SUB:j * Q_SUB + K_WIN, :]
            vw = vwin[res, j * Q_SUB:j * Q_SUB + K_WIN, :]
            g = sub0 + j
            variant = (g == 0).astype(jnp.int32) + 2 * (g == last_sub).astype(jnp.int32)
            s = lax.dot_general(qs_ref[...], kw, (((1,), (1,)), ((), ())), preferred_element_type=F32)
            s = s + bias_ref[variant]
            m = jnp.max(s, axis=-1, keepdims=True)
            p = jnp.exp(s - m)
            den = jnp.sum(p, axis=-1, keepdims=True)
            pb = p.astype(BF16)
            p_wide = jnp.concatenate([pb[h * Q_SUB:(h + 1) * Q_SUB] for h in range(HEADS_PER_GROUP)], axis=-1)
            v_tall = jnp.concatenate([vw * (head_of_lane == h).astype(BF16) for h in range(HEADS_PER_GROUP)], axis=0)
            o = jnp.dot(p_wide, v_tall, preferred_element_type=F32)
            lse = m + jnp.log(den)
            for half in range(2):
                cols = slice(half * LANES, (half + 1) * LANES)
                ra = slice(2 * half * Q_SUB, (2 * half + 1) * Q_SUB)
                rb = slice((2 * half + 1) * Q_SUB, (2 * half + 2) * Q_SUB)
                den_half = jnp.where(even_head, den[ra], den[rb])
                o_ref[res, rows, cols] = o[:, cols] / den_half
                lse_ref[res, rows, cols] = jnp.where(even_head, lse[ra], lse[rb])


def _attention_group(qkv, group):
    batch, dilation, sub_len, _ = qkv.shape
    tq = min(ATTN_ROWS_PER_STEP, sub_len)
    n_res = min(ATTN_ROWS_PER_STEP // tq, dilation)
    assert sub_len % tq == 0 and tq % Q_SUB == 0 and dilation % n_res == 0
    halo_per_tq = tq // N_SIDE
    n_halo = sub_len // N_SIDE
    slopes = tuple(float(np.float32(2.0 ** (-8.0 * (group * HEADS_PER_GROUP + h + 1) / N_ATTN_HEADS)))
                   for h in range(HEADS_PER_GROUP))

    def cur(part):
        return pl.BlockSpec((None, n_res, tq, GROUP_DIM), lambda b, r, i: (b, r, i, part))

    def prev(part):
        return pl.BlockSpec((None, n_res, N_SIDE, GROUP_DIM),
                            lambda b, r, i: (b, r, jnp.maximum(i * halo_per_tq - 1, 0), part))

    def nxt(part):
        return pl.BlockSpec((None, n_res, N_SIDE, GROUP_DIM),
                            lambda b, r, i: (b, r, jnp.minimum((i + 1) * halo_per_tq, n_halo - 1), part))

    out_spec = pl.BlockSpec((None, n_res, tq, GROUP_DIM), lambda b, r, i: (b, r, i, 0))
    out_shape = jax.ShapeDtypeStruct((batch, dilation, sub_len, GROUP_DIM), F32)
    return pl.pallas_call(
        functools.partial(_attn_kernel, dilation=dilation, slopes=slopes, sub_len=sub_len, tq=tq, n_res=n_res),
        grid=(batch, dilation // n_res, sub_len // tq),
        in_specs=[cur(0), prev(1), cur(1), nxt(1), prev(2), cur(2), nxt(2)],
        out_specs=[out_spec, out_spec],
        out_shape=[out_shape, out_shape],
        scratch_shapes=[pltpu.VMEM((n_res, tq + 2 * N_SIDE, GROUP_DIM), BF16),
                        pltpu.VMEM((n_res, tq + 2 * N_SIDE, GROUP_DIM), BF16),
                        pltpu.VMEM((HEADS_PER_GROUP * Q_SUB, GROUP_DIM), BF16),
                        pltpu.VMEM((4, HEADS_PER_GROUP * Q_SUB, K_WIN), F32)],
        compiler_params=_params(3),
        name=f"attn_d{dilation}",
    )(qkv, qkv, qkv, qkv, qkv, qkv, qkv)


def _pool_mean_minus_token(u_ref, up_ref, un_ref, ubuf, abuf, *, tm, seq):
    tiles_per_seq = seq // tm
    ti = pl.program_id(1)
    n = tm + 2 * POOL_HALO
    zero_halo = jnp.zeros((POOL_HALO, POOL_DIM), F32)
    ubuf[0:POOL_HALO] = jnp.where(ti == 0, zero_halo, up_ref[...])
    ubuf[POOL_HALO:POOL_HALO + tm] = u_ref[...]
    ubuf[POOL_HALO + tm:n] = jnp.where(ti == tiles_per_seq - 1, zero_halo, un_ref[...])
    ubuf[n:] = zero_halo
    abuf[n:] = zero_halo
    lo, hi = slice(0, LANES), slice(LANES, 2 * LANES)
    a2 = ubuf[0:n, lo] + ubuf[1:n + 1, lo]
    abuf[0:n, lo] = a2
    s2 = abuf[7:7 + tm, lo]
    ubuf[0:n, lo] = a2 + abuf[2:n + 2, lo]
    s4 = ubuf[6:6 + tm, lo]
    a2 = ubuf[0:n, hi] + ubuf[1:n + 1, hi]
    abuf[0:n, hi] = a2
    a4 = a2 + abuf[2:n + 2, hi]
    ubuf[0:n, hi] = a4
    a8 = a4 + ubuf[4:n + 4, hi]
    abuf[0:n, hi] = a8
    s8 = abuf[4:4 + tm, hi]
    s16 = a8[0:tm] + abuf[8:8 + tm, hi]
    first_group = lax.broadcasted_iota(jnp.int32, (1, LANES), 1) < POOL_GROUP_DIM
    t = ti * tm + lax.broadcasted_iota(jnp.int32, (tm, 1), 0)

    def mean(sum_a, sum_b, half_a, half_b):
        half = jnp.where(first_group, half_a, half_b)
        count = jnp.minimum(t + half, seq) - jnp.maximum(t - half, 0)
        return jnp.where(first_group, sum_a, sum_b) / count.astype(F32)

    return jnp.concatenate([mean(s2, s4, 1, 2), mean(s8, s16, 4, 8)], axis=-1) - u_ref[...]


def _token_order(blk_ref, buf, *, tm):
    dilation = blk_ref.shape[0]
    if dilation == 1:
        return blk_ref[0]
    n_slabs = GROUP_DIM // LANES
    part = tm // SPLIT_STRIDE
    if dilation == SPLIT_STRIDE:
        for r in range(dilation):
            for s in range(n_slabs):
                buf[0, s, pl.ds(r, part, stride=SPLIT_STRIDE), :] = blk_ref[r, :, s * LANES:(s + 1) * LANES]
        return jnp.concatenate([buf[0, s] for s in range(n_slabs)], axis=-1)
    assert dilation == SPLIT_STRIDE * SPLIT_STRIDE
    for r_lo in range(SPLIT_STRIDE):
        for r_hi in range(SPLIT_STRIDE):
            for s in range(n_slabs):
                buf[1, s, pl.ds(r_lo * part + r_hi, tm // dilation, stride=SPLIT_STRIDE), :] = (
                    blk_ref[r_lo + SPLIT_STRIDE * r_hi, :, s * LANES:(s + 1) * LANES])
    for r_lo in range(SPLIT_STRIDE):
        for s in range(n_slabs):
            buf[0, s, pl.ds(r_lo, part, stride=SPLIT_STRIDE), :] = buf[1, s, r_lo * part:(r_lo + 1) * part, :]
    return jnp.concatenate([buf[0, s] for s in range(n_slabs)], axis=-1)


def _mix_ffn_kernel(x_ref, u_ref, up_ref, un_ref, o1_ref, o2_ref, o3_ref, l1_ref, l2_ref, l3_ref,
                    wpool_ref, pscale_ref, wout_ref, gmix_ref, gpre_ref, wg_ref, wu_ref, wd_ref, gpost_ref,
                    out_ref, ubuf, abuf, ob2, ob3, lb2, lb3, acc_ref, *, tm, seq):
    y = _pool_mean_minus_token(u_ref, up_ref, un_ref, ubuf, abuf, tm=tm, seq=seq)
    a_pool = jnp.dot(y.astype(BF16), wpool_ref[...], preferred_element_type=F32) * pscale_ref[...]
    l1 = l1_ref[0]
    l2 = _token_order(l2_ref, lb2, tm=tm)
    l3 = _token_order(l3_ref, lb3, tm=tm)
    m = jnp.maximum(jnp.maximum(l1, l2), l3)
    e1, e2, e3 = jnp.exp(l1 - m), jnp.exp(l2 - m), jnp.exp(l3 - m)
    inv_z = 1.0 / (e1 + e2 + e3)
    o2 = _token_order(o2_ref, ob2, tm=tm)
    o3 = _token_order(o3_ref, ob3, tm=tm)
    cat = jnp.concatenate([a_pool, o1_ref[0] * (e1 * inv_z), o2 * (e2 * inv_z), o3 * (e3 * inv_z)],
                          axis=-1).astype(BF16)
    mix = jnp.dot(cat, wout_ref[...], preferred_element_type=F32)
    x2 = x_ref[...] + _rmsnorm(mix, gmix_ref[...])
    out_ref[...] = _ffn_tile(x2, gpre_ref, wg_ref, wu_ref, wd_ref, gpost_ref, acc_ref)


def _mix_ffn(x, u, attn, w_pool_bd, pool_scale, w_out, g_mix_post, g_pre, wg, wu, wd, g_post, *, tm):
    batch, seq, _ = x.shape
    n_halo = seq // POOL_HALO
    halo_per_tm = tm // POOL_HALO
    row = lambda width: pl.BlockSpec((None, tm, width), lambda b, i: (b, i, 0))
    prev = pl.BlockSpec((None, POOL_HALO, POOL_DIM), lambda b, i: (b, jnp.maximum(i * halo_per_tm - 1, 0), 0))
    nxt = pl.BlockSpec((None, POOL_HALO, POOL_DIM),
                       lambda b, i: (b, jnp.minimum((i + 1) * halo_per_tm, n_halo - 1), 0))
    by_residue = [pl.BlockSpec((None, d, tm // d, GROUP_DIM), lambda b, i: (b, 0, i, 0)) for _, d in ATTN_GROUPS]
    (o1, l1), (o2, l2), (o3, l3) = attn
    pool_buf = pltpu.VMEM((tm + 3 * POOL_HALO, POOL_DIM), F32)
    order_buf = pltpu.VMEM((2, GROUP_DIM // LANES, tm, LANES), F32)
    return pl.pallas_call(
        functools.partial(_mix_ffn_kernel, tm=tm, seq=seq),
        grid=(batch, seq // tm),
        in_specs=[row(D_MODEL), row(POOL_DIM), prev, nxt] + by_residue + by_residue +
                 [_const_spec((POOL_DIM, POOL_DIM)), _const_spec((1, POOL_DIM)),
                  _const_spec((D_MODEL, D_MODEL)), _const_spec((1, D_MODEL)), _const_spec((1, D_MODEL)),
                  _const_spec((D_MODEL, D_FF)), _const_spec((D_MODEL, D_FF)), _const_spec((D_FF, D_MODEL)),
                  _const_spec((1, D_MODEL))],
        out_specs=row(D_MODEL),
        out_shape=jax.ShapeDtypeStruct((batch, seq, D_MODEL), F32),
        scratch_shapes=[pool_buf, pool_buf, order_buf, order_buf, order_buf, order_buf,
                        pltpu.VMEM((tm, D_MODEL), F32)],
        compiler_params=_params(2),
        name="mix_ffn",
    )(x, u, u, u, o1, o2, o3, l1, l2, l3, w_pool_bd, pool_scale, w_out, g_mix_post, g_pre, wg, wu, wd, g_post)


def _cast_kernel(*refs):
    n = len(refs) // 2
    for src, dst in zip(refs[:n], refs[n:]):
        dst[...] = src[...].astype(dst.dtype)


def _ffn_weights(w_gate, w_up, w_down):
    ws = (w_gate, w_up, w_down)
    specs = [pl.BlockSpec((w.shape[0] // WEIGHT_CAST_STEPS, w.shape[1]), lambda i: (i, 0)) for w in ws]
    return pl.pallas_call(
        _cast_kernel,
        grid=(WEIGHT_CAST_STEPS,),
        in_specs=specs,
        out_specs=specs,
        out_shape=[jax.ShapeDtypeStruct(w.shape, BF16) for w in ws],
        compiler_params=_params(1),
        name="cast_ffn_weights",
    )(*ws)


def _cast_mix_kernel(win_ref, wout_ref, wpin_bf_ref, wqkv_bf_ref, wout_bf_ref):
    wpin_bf_ref[...] = win_ref[:, 0:POOL_DIM].astype(BF16)
    for gi in range(len(ATTN_GROUPS)):
        for part in range(3):
            src = POOL_DIM + part * ATTN_DIM + gi * GROUP_DIM
            dst = gi * QKV_GROUP_DIM + part * GROUP_DIM
            wqkv_bf_ref[:, dst:dst + GROUP_DIM] = win_ref[:, src:src + GROUP_DIM].astype(BF16)
    wout_bf_ref[...] = wout_ref[...].astype(BF16)


def _mix_weights(w_in, w_out):
    rows = D_MODEL // WEIGHT_CAST_STEPS
    spec = lambda width: pl.BlockSpec((rows, width), lambda i: (i, 0))
    return pl.pallas_call(
        _cast_mix_kernel,
        grid=(WEIGHT_CAST_STEPS,),
        in_specs=[spec(POOL_DIM + D_QKV), spec(D_MODEL)],
        out_specs=[spec(POOL_DIM), spec(D_QKV), spec(D_MODEL)],
        out_shape=[jax.ShapeDtypeStruct((D_MODEL, POOL_DIM), BF16), jax.ShapeDtypeStruct((D_MODEL, D_QKV), BF16),
                   jax.ShapeDtypeStruct((D_MODEL, D_MODEL), BF16)],
        compiler_params=_params(1),
        name="cast_mix_weights",
    )(w_in, w_out)


def _block_diag(w_lin):
    g, c, e = w_lin.shape
    eye = jnp.eye(g, dtype=w_lin.dtype)
    return (eye[:, None, :, None] * w_lin[:, :, None, :]).reshape(g * c, g * e)


def kernel(x, g_ffn1_pre, w1_gate, w1_up, w1_down, g_ffn1_post, g_mix_pre, w_in, w_pool_lin, pool_scale,
           w_out, g_mix_post, g_ffn2_pre, w2_gate, w2_up, w2_down, g_ffn2_post):
    batch, seq, _ = x.shape
    depth = g_ffn1_pre.shape[0]
    tm = 512
    assert seq % tm == 0 and tm % (ATTN_GROUPS[-1][1] * 16) == 0
    h = x
    for l in range(depth):
        w_pool_in, w_qkv, w_out_bf = _mix_weights(w_in[l], w_out[l])
        x1, u, *qkv = _ffn_proj(h, g_ffn1_pre[l][None], *_ffn_weights(w1_gate[l], w1_up[l], w1_down[l]),
                                g_ffn1_post[l][None], g_mix_pre[l][None], w_pool_in, w_qkv, tm=tm)
        attn = [_attention_group(qkv[gi], gi) for gi in range(len(ATTN_GROUPS))]
        h = _mix_ffn(x1, u, attn, _block_diag(w_pool_lin[l]).astype(BF16), pool_scale[l][None],
                     w_out_bf, g_mix_post[l][None], g_ffn2_pre[l][None],
                     *_ffn_weights(w2_gate[l], w2_up[l], w2_down[l]), g_ffn2_post[l][None], tm=tm)
    return h
```

```python
import functools

import numpy as np
import jax
import jax.numpy as jnp
from jax import lax
from jax.experimental import pallas as pl
from jax.experimental.pallas import tpu as pltpu

D_MODEL = 1024
HEAD_DIM = 64
POOL_WINDOWS = (2, 4, 8, 16)
POOL_GROUP_DIM = 64
POOL_DIM = len(POOL_WINDOWS) * POOL_GROUP_DIM
ATTN_GROUPS = ((128, 1), (512, 4), (2048, 16))
HEADS_PER_GROUP = 4
N_ATTN_HEADS = HEADS_PER_GROUP * len(ATTN_GROUPS)
GROUP_DIM = HEADS_PER_GROUP * HEAD_DIM
ATTN_DIM = N_ATTN_HEADS * HEAD_DIM
D_QKV = 3 * ATTN_DIM
QKV_GROUP_DIM = 3 * GROUP_DIM
D_FF = 2752
RMS_EPS = 1e-6

LANES = 128
V7X_MXU_DIM = 256
V7X_VMEM_LIMIT_BYTES = 56 * 1024 * 1024
N_SIDE = 64
Q_SUB = 128
K_WIN = Q_SUB + 2 * N_SIDE
ATTN_ROWS_PER_STEP = 2048
SPLIT_STRIDE = 4
WEIGHT_CAST_STEPS = 4
POOL_HALO = 8
MASKED = -1e30

F32 = jnp.float32
BF16 = jnp.bfloat16


def _rmsnorm(x, g):
    ms = jnp.mean(x * x, axis=-1, keepdims=True)
    return x * lax.rsqrt(ms + RMS_EPS) * g


def _const_spec(shape):
    return pl.BlockSpec(shape, lambda *_: (0,) * len(shape), pipeline_mode=pl.Buffered(1))


def _params(n_axes):
    return pltpu.CompilerParams(dimension_semantics=("arbitrary",) * n_axes,
                                vmem_limit_bytes=V7X_VMEM_LIMIT_BYTES)


def _ffn_tile(x, gpre_ref, wg_ref, wu_ref, wd_ref, gpost_ref, acc_ref):
    xn = _rmsnorm(x, gpre_ref[...]).astype(BF16)
    for c0 in range(0, D_FF, V7X_MXU_DIM):
        c1 = min(c0 + V7X_MXU_DIM, D_FF)
        gate = jnp.dot(xn, wg_ref[:, c0:c1], preferred_element_type=F32)
        up = jnp.dot(xn, wu_ref[:, c0:c1], preferred_element_type=F32)
        act = (gate * jax.nn.sigmoid(gate) * up).astype(BF16)
        part = jnp.dot(act, wd_ref[c0:c1, :], preferred_element_type=F32)
        if c0 == 0:
            acc_ref[...] = part
        else:
            acc_ref[...] += part
    return x + 0.5 * _rmsnorm(acc_ref[...], gpost_ref[...])


def _ffn_proj_kernel(x_ref, gpre_ref, wg_ref, wu_ref, wd_ref, gpost_ref, gmix_ref, wpin_ref, wqkv_ref,
                     x1_ref, u_ref, q0_ref, q1_ref, q2_ref, acc_ref, zbuf, *, tm):
    x1 = _ffn_tile(x_ref[...], gpre_ref, wg_ref, wu_ref, wd_ref, gpost_ref, acc_ref)
    x1_ref[...] = x1
    h = _rmsnorm(x1, gmix_ref[...]).astype(BF16)
    u_ref[...] = jnp.dot(h, wpin_ref[...], preferred_element_type=F32)
    n_slabs = QKV_GROUP_DIM // LANES
    for gi, (out, (_, dilation)) in enumerate(zip((q0_ref, q1_ref, q2_ref), ATTN_GROUPS)):
        z = jnp.dot(h, wqkv_ref[:, gi * QKV_GROUP_DIM:(gi + 1) * QKV_GROUP_DIM], preferred_element_type=F32)
        if dilation == 1:
            out[0] = z.astype(BF16)
            continue
        for s in range(n_slabs):
            zbuf[0, s] = z[:, s * LANES:(s + 1) * LANES]
        part = tm // SPLIT_STRIDE
        if dilation == SPLIT_STRIDE:
            for r in range(dilation):
                for s in range(n_slabs):
                    out[r, :, s * LANES:(s + 1) * LANES] = (
                        zbuf[0, s, pl.ds(r, part, stride=SPLIT_STRIDE), :].astype(BF16))
            continue
        assert dilation == SPLIT_STRIDE * SPLIT_STRIDE
        for r_lo in range(SPLIT_STRIDE):
            for s in range(n_slabs):
                zbuf[1, s, r_lo * part:(r_lo + 1) * part, :] = zbuf[0, s, pl.ds(r_lo, part, stride=SPLIT_STRIDE), :]
        for r_lo in range(SPLIT_STRIDE):
            for r_hi in range(SPLIT_STRIDE):
                for s in range(n_slabs):
                    out[r_lo + SPLIT_STRIDE * r_hi, :, s * LANES:(s + 1) * LANES] = (
                        zbuf[1, s, pl.ds(r_lo * part + r_hi, tm // dilation, stride=SPLIT_STRIDE), :].astype(BF16))


def _ffn_proj(x, g_pre, wg, wu, wd, g_post, g_mix, w_pool_in, w_qkv, *, tm):
    batch, seq, _ = x.shape
    row = lambda width: pl.BlockSpec((None, tm, width), lambda b, i: (b, i, 0))
    qkv_specs = [pl.BlockSpec((None, d, tm // d, QKV_GROUP_DIM), lambda b, i: (b, 0, i, 0)) for _, d in ATTN_GROUPS]
    qkv_shapes = [jax.ShapeDtypeStruct((batch, d, seq // d, QKV_GROUP_DIM), BF16) for _, d in ATTN_GROUPS]
    return pl.pallas_call(
        functools.partial(_ffn_proj_kernel, tm=tm),
        grid=(batch, seq // tm),
        in_specs=[row(D_MODEL), _const_spec((1, D_MODEL)), _const_spec((D_MODEL, D_FF)),
                  _const_spec((D_MODEL, D_FF)), _const_spec((D_FF, D_MODEL)), _const_spec((1, D_MODEL)),
                  _const_spec((1, D_MODEL)), _const_spec((D_MODEL, POOL_DIM)), _const_spec((D_MODEL, D_QKV))],
        out_specs=[row(D_MODEL), row(POOL_DIM)] + qkv_specs,
        out_shape=[jax.ShapeDtypeStruct((batch, seq, D_MODEL), F32),
                   jax.ShapeDtypeStruct((batch, seq, POOL_DIM), F32)] + qkv_shapes,
        scratch_shapes=[pltpu.VMEM((tm, D_MODEL), F32),
                        pltpu.VMEM((len(ATTN_GROUPS) - 1, QKV_GROUP_DIM // LANES, tm, LANES), F32)],
        compiler_params=_params(2),
        name="ffn_proj",
    )(x, g_pre, wg, wu, wd, g_post, g_mix, w_pool_in, w_qkv)


def _attn_kernel(q_ref, kp_ref, kc_ref, kn_ref, vp_ref, vc_ref, vn_ref, o_ref, lse_ref,
                 kwin, vwin, qs_ref, bias_ref, *, dilation, slopes, sub_len, tq, n_res):
    first = (pl.program_id(0) == 0) & (pl.program_id(1) == 0) & (pl.program_id(2) == 0)

    @pl.when(first)
    def _():
        qi = lax.broadcasted_iota(jnp.int32, (Q_SUB, K_WIN), 0)
        kj = lax.broadcasted_iota(jnp.int32, (Q_SUB, K_WIN), 1)
        rel = jnp.abs(kj - N_SIDE - qi)
        dist = (rel * dilation).astype(F32)
        in_band = rel <= N_SIDE
        for variant in range(4):
            ok = in_band
            if variant & 1:
                ok = ok & (kj >= N_SIDE)
            if variant & 2:
                ok = ok & (kj < N_SIDE + Q_SUB)
            for h in range(HEADS_PER_GROUP):
                bias_ref[variant, h * Q_SUB:(h + 1) * Q_SUB, :] = jnp.where(ok, -slopes[h] * dist, MASKED)

    kwin[:, 0:N_SIDE] = kp_ref[...]
    kwin[:, N_SIDE:N_SIDE + tq] = kc_ref[...]
    kwin[:, N_SIDE + tq:] = kn_ref[...]
    vwin[:, 0:N_SIDE] = vp_ref[...]
    vwin[:, N_SIDE:N_SIDE + tq] = vc_ref[...]
    vwin[:, N_SIDE + tq:] = vn_ref[...]

    n_sub = tq // Q_SUB
    sub0 = pl.program_id(2) * n_sub
    last_sub = sub_len // Q_SUB - 1
    lane = lax.broadcasted_iota(jnp.int32, (1, GROUP_DIM), 1)
    head_of_lane = lane >> 6
    scale = HEAD_DIM ** -0.5
    even_head = lax.broadcasted_iota(jnp.int32, (1, LANES), 1) < HEAD_DIM

    for res in range(n_res):
        for j in range(n_sub):
            rows = slice(j * Q_SUB, (j + 1) * Q_SUB)
            q = q_ref[res, rows, :]
            for h in range(HEADS_PER_GROUP):
                qs_ref[h * Q_SUB:(h + 1) * Q_SUB, :] = q * jnp.where(head_of_lane == h, scale, 0.0).astype(BF16)
            kw = kwin[res, j * Q_SUB:j * Q_SUB + K_WIN, :]
            vw = vwin[res, j * Q_SUB:j * Q_SUB + K_WIN, :]
            g = sub0 + j
            variant = (g == 0).astype(jnp.int32) + 2 * (g == last_sub).astype(jnp.int32)
            s = lax.dot_general(qs_ref[...], kw, (((1,), (1,)), ((), ())), preferred_element_type=F32)
            s = s + bias_ref[variant]
            m = jnp.max(s, axis=-1, keepdims=True)
            p = jnp.exp(s - m)
            den = jnp.sum(p, axis=-1, keepdims=True)
            pb = p.astype(BF16)
            p_wide = jnp.concatenate([pb[h * Q_SUB:(h + 1) * Q_SUB] for h in range(HEADS_PER_GROUP)], axis=-1)
            v_tall = jnp.concatenate([vw * (head_of_lane == h).astype(BF16) for h in range(HEADS_PER_GROUP)], axis=0)
            o = jnp.dot(p_wide, v_tall, preferred_element_type=F32)
            lse = m + jnp.log(den)
            for half in range(2):
                cols = slice(half * LANES, (half + 1) * LANES)
                ra = slice(2 * half * Q_SUB, (2 * half + 1) * Q_SUB)
                rb = slice((2 * half + 1) * Q_SUB, (2 * half + 2) * Q_SUB)
                den_half = jnp.where(even_head, den[ra], den[rb])
                o_ref[res, rows, cols] = o[:, cols] / den_half
                lse_ref[res, rows, cols] = jnp.where(even_head, lse[ra], lse[rb])


def _attention_group(qkv, group):
    batch, dilation, sub_len, _ = qkv.shape
    tq = min(ATTN_ROWS_PER_STEP, sub_len)
    n_res = min(ATTN_ROWS_PER_STEP // tq, dilation)
    assert sub_len % tq == 0 and tq % Q_SUB == 0 and dilation % n_res == 0
    halo_per_tq = tq // N_SIDE
    n_halo = sub_len // N_SIDE
    slopes = tuple(float(np.float32(2.0 ** (-8.0 * (group * HEADS_PER_GROUP + h + 1) / N_ATTN_HEADS)))
                   for h in range(HEADS_PER_GROUP))

    def cur(part):
        return pl.BlockSpec((None, n_res, tq, GROUP_DIM), lambda b, r, i: (b, r, i, part))

    def prev(part):
        return pl.BlockSpec((None, n_res, N_SIDE, GROUP_DIM),
                            lambda b, r, i: (b, r, jnp.maximum(i * halo_per_tq - 1, 0), part))

    def nxt(part):
        return pl.BlockSpec((None, n_res, N_SIDE, GROUP_DIM),
                            lambda b, r, i: (b, r, jnp.minimum((i + 1) * halo_per_tq, n_halo - 1), part))

    out_spec = pl.BlockSpec((None, n_res, tq, GROUP_DIM), lambda b, r, i: (b, r, i, 0))
    out_shape = jax.ShapeDtypeStruct((batch, dilation, sub_len, GROUP_DIM), F32)
    return pl.pallas_call(
        functools.partial(_attn_kernel, dilation=dilation, slopes=slopes, sub_len=sub_len, tq=tq, n_res=n_res),
        grid=(batch, dilation // n_res, sub_len // tq),
        in_specs=[cur(0), prev(1), cur(1), nxt(1), prev(2), cur(2), nxt(2)],
        out_specs=[out_spec, out_spec],
        out_shape=[out_shape, out_shape],
        scratch_shapes=[pltpu.VMEM((n_res, tq + 2 * N_SIDE, GROUP_DIM), BF16),
                        pltpu.VMEM((n_res, tq + 2 * N_SIDE, GROUP_DIM), BF16),
                        pltpu.VMEM((HEADS_PER_GROUP * Q_SUB, GROUP_DIM), BF16),
                        pltpu.VMEM((4, HEADS_PER_GROUP * Q_SUB, K_WIN), F32)],
        compiler_params=_params(3),
        name=f"attn_d{dilation}",
    )(qkv, qkv, qkv, qkv, qkv, qkv, qkv)


def _pool_mean_minus_token(u_ref, up_ref, un_ref, ubuf, abuf, *, tm, seq):
    tiles_per_seq = seq // tm
    ti = pl.program_id(1)
    n = tm + 2 * POOL_HALO
    zero_halo = jnp.zeros((POOL_HALO, POOL_DIM), F32)
    ubuf[0:POOL_HALO] = jnp.where(ti == 0, zero_halo, up_ref[...])
    ubuf[POOL_HALO:POOL_HALO + tm] = u_ref[...]
    ubuf[POOL_HALO + tm:n] = jnp.where(ti == tiles_per_seq - 1, zero_halo, un_ref[...])
    ubuf[n:] = zero_halo
    abuf[n:] = zero_halo
    lo, hi = slice(0, LANES), slice(LANES, 2 * LANES)
    a2 = ubuf[0:n, lo] + ubuf[1:n + 1, lo]
    abuf[0:n, lo] = a2
    s2 = abuf[7:7 + tm, lo]
    ubuf[0:n, lo] = a2 + abuf[2:n + 2, lo]
    s4 = ubuf[6:6 + tm, lo]
    a2 = ubuf[0:n, hi] + ubuf[1:n + 1, hi]
    abuf[0:n, hi] = a2
    a4 = a2 + abuf[2:n + 2, hi]
    ubuf[0:n, hi] = a4
    a8 = a4 + ubuf[4:n + 4, hi]
    abuf[0:n, hi] = a8
    s8 = abuf[4:4 + tm, hi]
    s16 = a8[0:tm] + abuf[8:8 + tm, hi]
    first_group = lax.broadcasted_iota(jnp.int32, (1, LANES), 1) < POOL_GROUP_DIM
    t = ti * tm + lax.broadcasted_iota(jnp.int32, (tm, 1), 0)

    def mean(sum_a, sum_b, half_a, half_b):
        half = jnp.where(first_group, half_a, half_b)
        count = jnp.minimum(t + half, seq) - jnp.maximum(t - half, 0)
        return jnp.where(first_group, sum_a, sum_b) / count.astype(F32)

    return jnp.concatenate([mean(s2, s4, 1, 2), mean(s8, s16, 4, 8)], axis=-1) - u_ref[...]


def _token_order(blk_ref, buf, *, tm):
    dilation = blk_ref.shape[0]
    if dilation == 1:
        return blk_ref[0]
    n_slabs = GROUP_DIM // LANES
    part = tm // SPLIT_STRIDE
    if dilation == SPLIT_STRIDE:
        for r in range(dilation):
            for s in range(n_slabs):
                buf[0, s, pl.ds(r, part, stride=SPLIT_STRIDE), :] = blk_ref[r, :, s * LANES:(s + 1) * LANES]
        return jnp.concatenate([buf[0, s] for s in range(n_slabs)], axis=-1)
    assert dilation == SPLIT_STRIDE * SPLIT_STRIDE
    for r_lo in range(SPLIT_STRIDE):
        for r_hi in range(SPLIT_STRIDE):
            for s in range(n_slabs):
                buf[1, s, pl.ds(r_lo * part + r_hi, tm // dilation, stride=SPLIT_STRIDE), :] = (
                    blk_ref[r_lo + SPLIT_STRIDE * r_hi, :, s * LANES:(s + 1) * LANES])
    for r_lo in range(SPLIT_STRIDE):
        for s in range(n_slabs):
            buf[0, s, pl.ds(r_lo, part, stride=SPLIT_STRIDE), :] = buf[1, s, r_lo * part:(r_lo + 1) * part, :]
    return jnp.concatenate([buf[0, s] for s in range(n_slabs)], axis=-1)


def _mix_ffn_kernel(x_ref, u_ref, up_ref, un_ref, o1_ref, o2_ref, o3_ref, l1_ref, l2_ref, l3_ref,
                    wpool_ref, pscale_ref, wout_ref, gmix_ref, gpre_ref, wg_ref, wu_ref, wd_ref, gpost_ref,
                    out_ref, ubuf, abuf, ob2, ob3, lb2, lb3, acc_ref, *, tm, seq):
    y = _pool_mean_minus_token(u_ref, up_ref, un_ref, ubuf, abuf, tm=tm, seq=seq)
    a_pool = jnp.dot(y.astype(BF16), wpool_ref[...], preferred_element_type=F32) * pscale_ref[...]
    l1 = l1_ref[0]
    l2 = _token_order(l2_ref, lb2, tm=tm)
    l3 = _token_order(l3_ref, lb3, tm=tm)
    m = jnp.maximum(jnp.maximum(l1, l2), l3)
    e1, e2, e3 = jnp.exp(l1 - m), jnp.exp(l2 - m), jnp.exp(l3 - m)
    inv_z = 1.0 / (e1 + e2 + e3)
    o2 = _token_order(o2_ref, ob2, tm=tm)
    o3 = _token_order(o3_ref, ob3, tm=tm)
    cat = jnp.concatenate([a_pool, o1_ref[0] * (e1 * inv_z), o2 * (e2 * inv_z), o3 * (e3 * inv_z)],
                          axis=-1).astype(BF16)
    mix = jnp.dot(cat, wout_ref[...], preferred_element_type=F32)
    x2 = x_ref[...] + _rmsnorm(mix, gmix_ref[...])
    out_ref[...] = _ffn_tile(x2, gpre_ref, wg_ref, wu_ref, wd_ref, gpost_ref, acc_ref)


def _mix_ffn(x, u, attn, w_pool_bd, pool_scale, w_out, g_mix_post, g_pre, wg, wu, wd, g_post, *, tm):
    batch, seq, _ = x.shape
    n_halo = seq // POOL_HALO
    halo_per_tm = tm // POOL_HALO
    row = lambda width: pl.BlockSpec((None, tm, width), lambda b, i: (b, i, 0))
    prev = pl.BlockSpec((None, POOL_HALO, POOL_DIM), lambda b, i: (b, jnp.maximum(i * halo_per_tm - 1, 0), 0))
    nxt = pl.BlockSpec((None, POOL_HALO, POOL_DIM),
                       lambda b, i: (b, jnp.minimum((i + 1) * halo_per_tm, n_halo - 1), 0))
    by_residue = [pl.BlockSpec((None, d, tm // d, GROUP_DIM), lambda b, i: (b, 0, i, 0)) for _, d in ATTN_GROUPS]
    (o1, l1), (o2, l2), (o3, l3) = attn
    pool_buf = pltpu.VMEM((tm + 3 * POOL_HALO, POOL_DIM), F32)
    order_buf = pltpu.VMEM((2, GROUP_DIM // LANES, tm, LANES), F32)
    return pl.pallas_call(
        functools.partial(_mix_ffn_kernel, tm=tm, seq=seq),
        grid=(batch, seq // tm),
        in_specs=[row(D_MODEL), row(POOL_DIM), prev, nxt] + by_residue + by_residue +
                 [_const_spec((POOL_DIM, POOL_DIM)), _const_spec((1, POOL_DIM)),
                  _const_spec((D_MODEL, D_MODEL)), _const_spec((1, D_MODEL)), _const_spec((1, D_MODEL)),
                  _const_spec((D_MODEL, D_FF)), _const_spec((D_MODEL, D_FF)), _const_spec((D_FF, D_MODEL)),
                  _const_spec((1, D_MODEL))],
        out_specs=row(D_MODEL),
        out_shape=jax.ShapeDtypeStruct((batch, seq, D_MODEL), F32),
        scratch_shapes=[pool_buf, pool_buf, order_buf, order_buf, order_buf, order_buf,
                        pltpu.VMEM((tm, D_MODEL), F32)],
        compiler_params=_params(2),
        name="mix_ffn",
    )(x, u, u, u, o1, o2, o3, l1, l2, l3, w_pool_bd, pool_scale, w_out, g_mix_post, g_pre, wg, wu, wd, g_post)


def _cast_kernel(*refs):
    n = len(refs) // 2
    for src, dst in zip(refs[:n], refs[n:]):
        dst[...] = src[...].astype(dst.dtype)


def _ffn_weights(w_gate, w_up, w_down, layer):
    ws = (w_gate, w_up, w_down)
    blocks = [(w.shape[1] // WEIGHT_CAST_STEPS, w.shape[2]) for w in ws]
    return pl.pallas_call(
        _cast_kernel,
        grid=(WEIGHT_CAST_STEPS,),
        in_specs=[pl.BlockSpec((None,) + blk, lambda i: (layer, i, 0)) for blk in blocks],
        out_specs=[pl.BlockSpec(blk, lambda i: (i, 0)) for blk in blocks],
        out_shape=[jax.ShapeDtypeStruct(w.shape[1:], BF16) for w in ws],
        compiler_params=_params(1),
        name="cast_ffn_weights",
    )(*ws)


def _cast_mix_kernel(win_ref, wout_ref, wpin_bf_ref, wqkv_bf_ref, wout_bf_ref):
    wpin_bf_ref[...] = win_ref[:, 0:POOL_DIM].astype(BF16)
    for gi in range(len(ATTN_GROUPS)):
        for part in range(3):
            src = POOL_DIM + part * ATTN_DIM + gi * GROUP_DIM
            dst = gi * QKV_GROUP_DIM + part * GROUP_DIM
            wqkv_bf_ref[:, dst:dst + GROUP_DIM] = win_ref[:, src:src + GROUP_DIM].astype(BF16)
    wout_bf_ref[...] = wout_ref[...].astype(BF16)


def _mix_weights(w_in, w_out):
    rows = D_MODEL // WEIGHT_CAST_STEPS
    spec = lambda width: pl.BlockSpec((rows, width), lambda i: (i, 0))
    return pl.pallas_call(
        _cast_mix_kernel,
        grid=(WEIGHT_CAST_STEPS,),
        in_specs=[spec(POOL_DIM + D_QKV), spec(D_MODEL)],
        out_specs=[spec(POOL_DIM), spec(D_QKV), spec(D_MODEL)],
        out_shape=[jax.ShapeDtypeStruct((D_MODEL, POOL_DIM), BF16), jax.ShapeDtypeStruct((D_MODEL, D_QKV), BF16),
                   jax.ShapeDtypeStruct((D_MODEL, D_MODEL), BF16)],
        compiler_params=_params(1),
        name="cast_mix_weights",
    )(w_in, w_out)


def _block_diag(w_lin):
    g, c, e = w_lin.shape
    eye = jnp.eye(g, dtype=w_lin.dtype)
    return (eye[:, None, :, None] * w_lin[:, :, None, :]).reshape(g * c, g * e)


def kernel(x, g_ffn1_pre, w1_gate, w1_up, w1_down, g_ffn1_post, g_mix_pre, w_in, w_pool_lin, pool_scale,
           w_out, g_mix_post, g_ffn2_pre, w2_gate, w2_up, w2_down, g_ffn2_post):
    batch, seq, _ = x.shape
    depth = g_ffn1_pre.shape[0]
    tm = 512
    assert seq % tm == 0 and tm % (ATTN_GROUPS[-1][1] * 16) == 0
    h = x
    for l in range(depth):
        w_pool_in, w_qkv, w_out_bf = _mix_weights(w_in[l], w_out[l])
        x1, u, *qkv = _ffn_proj(h, g_ffn1_pre[l][None], *_ffn_weights(w1_gate, w1_up, w1_down, l),
                                g_ffn1_post[l][None], g_mix_pre[l][None], w_pool_in, w_qkv, tm=tm)
        attn = [_attention_group(qkv[gi], gi) for gi in range(len(ATTN_GROUPS))]
        h = _mix_ffn(x1, u, attn, _block_diag(w_pool_lin[l]).astype(BF16), pool_scale[l][None],
                     w_out_bf, g_mix_post[l][None], g_ffn2_pre[l][None],
                     *_ffn_weights(w2_gate, w2_up, w2_down, l), g_ffn2_post[l][None], tm=tm)
    return h
```

```python
import functools

import numpy as np
import jax
import jax.numpy as jnp
from jax import lax
from jax.experimental import pallas as pl
from jax.experimental.pallas import tpu as pltpu

D_MODEL = 1024
HEAD_DIM = 64
POOL_WINDOWS = (2, 4, 8, 16)
POOL_GROUP_DIM = 64
POOL_DIM = len(POOL_WINDOWS) * POOL_GROUP_DIM
ATTN_GROUPS = ((128, 1), (512, 4), (2048, 16))
HEADS_PER_GROUP = 4
N_ATTN_HEADS = HEADS_PER_GROUP * len(ATTN_GROUPS)
GROUP_DIM = HEADS_PER_GROUP * HEAD_DIM
ATTN_DIM = N_ATTN_HEADS * HEAD_DIM
D_QKV = 3 * ATTN_DIM
QKV_GROUP_DIM = 3 * GROUP_DIM
D_FF = 2752
RMS_EPS = 1e-6

LANES = 128
V7X_MXU_DIM = 256
V7X_VMEM_LIMIT_BYTES = 56 * 1024 * 1024
N_SIDE = 64
Q_SUB = 128
K_WIN = Q_SUB + 2 * N_SIDE
ATTN_ROWS_PER_STEP = 4096
SPLIT_STRIDE = 4
POOL_HALO = 8
MASKED = -1e30

F32 = jnp.float32
BF16 = jnp.bfloat16


def _rmsnorm(x, g):
    ms = jnp.mean(x * x, axis=-1, keepdims=True)
    return x * lax.rsqrt(ms + RMS_EPS) * g


def _const_spec(shape):
    return pl.BlockSpec(shape, lambda *_: (0,) * len(shape), pipeline_mode=pl.Buffered(1))


def _params(n_axes):
    return pltpu.CompilerParams(dimension_semantics=("arbitrary",) * n_axes,
                                vmem_limit_bytes=V7X_VMEM_LIMIT_BYTES)


def _ffn_tile(x, gpre_ref, wg_ref, wu_ref, wd_ref, gpost_ref, acc_ref):
    xn = _rmsnorm(x, gpre_ref[...]).astype(BF16)
    for c0 in range(0, D_FF, V7X_MXU_DIM):
        c1 = min(c0 + V7X_MXU_DIM, D_FF)
        gate = jnp.dot(xn, wg_ref[:, c0:c1], preferred_element_type=F32)
        up = jnp.dot(xn, wu_ref[:, c0:c1], preferred_element_type=F32)
        act = (gate * jax.nn.sigmoid(gate) * up).astype(BF16)
        part = jnp.dot(act, wd_ref[c0:c1, :], preferred_element_type=F32)
        if c0 == 0:
            acc_ref[...] = part
        else:
            acc_ref[...] += part
    return x + 0.5 * _rmsnorm(acc_ref[...], gpost_ref[...])


def _ffn_proj_kernel(x_ref, gpre_ref, wg_ref, wu_ref, wd_ref, gpost_ref, gmix_ref, wpin_ref, wqkv_ref,
                     x1_ref, u_ref, q0_ref, q1_ref, q2_ref, acc_ref, zbuf, *, tm):
    x1 = _ffn_tile(x_ref[...], gpre_ref, wg_ref, wu_ref, wd_ref, gpost_ref, acc_ref)
    x1_ref[...] = x1
    h = _rmsnorm(x1, gmix_ref[...]).astype(BF16)
    u_ref[...] = jnp.dot(h, wpin_ref[...], preferred_element_type=F32)
    n_slabs = QKV_GROUP_DIM // LANES
    for gi, (out, (_, dilation)) in enumerate(zip((q0_ref, q1_ref, q2_ref), ATTN_GROUPS)):
        z = jnp.dot(h, wqkv_ref[:, gi * QKV_GROUP_DIM:(gi + 1) * QKV_GROUP_DIM], preferred_element_type=F32)
        if dilation == 1:
            out[0] = z.astype(BF16)
            continue
        for s in range(n_slabs):
            zbuf[0, s] = z[:, s * LANES:(s + 1) * LANES]
        part = tm // SPLIT_STRIDE
        if dilation == SPLIT_STRIDE:
            for r in range(dilation):
                for s in range(n_slabs):
                    out[r, :, s * LANES:(s + 1) * LANES] = (
                        zbuf[0, s, pl.ds(r, part, stride=SPLIT_STRIDE), :].astype(BF16))
            continue
        assert dilation == SPLIT_STRIDE * SPLIT_STRIDE
        for r_lo in range(SPLIT_STRIDE):
            for s in range(n_slabs):
                zbuf[1, s, r_lo * part:(r_lo + 1) * part, :] = zbuf[0, s, pl.ds(r_lo, part, stride=SPLIT_STRIDE), :]
        for r_lo in range(SPLIT_STRIDE):
            for r_hi in range(SPLIT_STRIDE):
                for s in range(n_slabs):
                    out[r_lo + SPLIT_STRIDE * r_hi, :, s * LANES:(s + 1) * LANES] = (
                        zbuf[1, s, pl.ds(r_lo * part + r_hi, tm // dilation, stride=SPLIT_STRIDE), :].astype(BF16))


def _ffn_proj(x, g_pre, wg, wu, wd, g_post, g_mix, w_pool_in, w_qkv, *, tm):
    batch, seq, _ = x.shape
    row = lambda width: pl.BlockSpec((None, tm, width), lambda b, i: (b, i, 0))
    qkv_specs = [pl.BlockSpec((None, d, tm // d, QKV_GROUP_DIM), lambda b, i: (b, 0, i, 0)) for _, d in ATTN_GROUPS]
    qkv_shapes = [jax.ShapeDtypeStruct((batch, d, seq // d, QKV_GROUP_DIM), BF16) for _, d in ATTN_GROUPS]
    return pl.pallas_call(
        functools.partial(_ffn_proj_kernel, tm=tm),
        grid=(batch, seq // tm),
        in_specs=[row(D_MODEL), _const_spec((1, D_MODEL)), _const_spec((D_MODEL, D_FF)),
                  _const_spec((D_MODEL, D_FF)), _const_spec((D_FF, D_MODEL)), _const_spec((1, D_MODEL)),
                  _const_spec((1, D_MODEL)), _const_spec((D_MODEL, POOL_DIM)), _const_spec((D_MODEL, D_QKV))],
        out_specs=[row(D_MODEL), row(POOL_DIM)] + qkv_specs,
        out_shape=[jax.ShapeDtypeStruct((batch, seq, D_MODEL), F32),
                   jax.ShapeDtypeStruct((batch, seq, POOL_DIM), F32)] + qkv_shapes,
        scratch_shapes=[pltpu.VMEM((tm, D_MODEL), F32),
                        pltpu.VMEM((len(ATTN_GROUPS) - 1, QKV_GROUP_DIM // LANES, tm, LANES), F32)],
        compiler_params=_params(2),
        name="ffn_proj",
    )(x, g_pre, wg, wu, wd, g_post, g_mix, w_pool_in, w_qkv)


def _attn_kernel(q_ref, kp_ref, kc_ref, kn_ref, vp_ref, vc_ref, vn_ref, o_ref, lse_ref,
                 kwin, vwin, qs_ref, bias_ref, *, dilation, slopes, sub_len, tq, n_res):
    first = (pl.program_id(0) == 0) & (pl.program_id(1) == 0) & (pl.program_id(2) == 0)

    @pl.when(first)
    def _():
        qi = lax.broadcasted_iota(jnp.int32, (Q_SUB, K_WIN), 0)
        kj = lax.broadcasted_iota(jnp.int32, (Q_SUB, K_WIN), 1)
        rel = jnp.abs(kj - N_SIDE - qi)
        dist = (rel * dilation).astype(F32)
        in_band = rel <= N_SIDE
        for variant in range(4):
            ok = in_band
            if variant & 1:
                ok = ok & (kj >= N_SIDE)
            if variant & 2:
                ok = ok & (kj < N_SIDE + Q_SUB)
            for h in range(HEADS_PER_GROUP):
                bias_ref[variant, h * Q_SUB:(h + 1) * Q_SUB, :] = jnp.where(ok, -slopes[h] * dist, MASKED)

    kwin[:, 0:N_SIDE] = kp_ref[...]
    kwin[:, N_SIDE:N_SIDE + tq] = kc_ref[...]
    kwin[:, N_SIDE + tq:] = kn_ref[...]
    vwin[:, 0:N_SIDE] = vp_ref[...]
    vwin[:, N_SIDE:N_SIDE + tq] = vc_ref[...]
    vwin[:, N_SIDE + tq:] = vn_ref[...]

    n_sub = tq // Q_SUB
    sub0 = pl.program_id(2) * n_sub
    last_sub = sub_len // Q_SUB - 1
    lane = lax.broadcasted_iota(jnp.int32, (1, GROUP_DIM), 1)
    head_of_lane = lane >> 6
    scale = HEAD_DIM ** -0.5
    even_head = lax.broadcasted_iota(jnp.int32, (1, LANES), 1) < HEAD_DIM

    for res in range(n_res):
        for j in range(n_sub):
            rows = slice(j * Q_SUB, (j + 1) * Q_SUB)
            q = q_ref[res, rows, :]
            for h in range(HEADS_PER_GROUP):
                qs_ref[h * Q_SUB:(h + 1) * Q_SUB, :] = q * jnp.where(head_of_lane == h, scale, 0.0).astype(BF16)
            kw = kwin[res, j * Q_SUB:j * Q_SUB + K_WIN, :]
            vw = vwin[res, j * Q_SUB:j * Q_SUB + K_WIN, :]
            g = sub0 + j
            variant = (g == 0).astype(jnp.int32) + 2 * (g == last_sub).astype(jnp.int32)
            s = lax.dot_general(qs_ref[...], kw, (((1,), (1,)), ((), ())), preferred_element_type=F32)
            s = s + bias_ref[variant]
            m = jnp.max(s, axis=-1, keepdims=True)
            p = jnp.exp(s - m)
            den = jnp.sum(p, axis=-1, keepdims=True)
            pb = p.astype(BF16)
            p_wide = jnp.concatenate([pb[h * Q_SUB:(h + 1) * Q_SUB] for h in range(HEADS_PER_GROUP)], axis=-1)
            v_tall = jnp.concatenate([vw * (head_of_lane == h).astype(BF16) for h in range(HEADS_PER_GROUP)], axis=0)
            o = jnp.dot(p_wide, v_tall, preferred_element_type=F32)
            lse = m + jnp.log(den)
            for half in range(2):
                cols = slice(half * LANES, (half + 1) * LANES)
                ra = slice(2 * half * Q_SUB, (2 * half + 1) * Q_SUB)
                rb = slice((2 * half + 1) * Q_SUB, (2 * half + 2) * Q_SUB)
                den_half = jnp.where(even_head, den[ra], den[rb])
                o_ref[res, rows, cols] = o[:, cols] / den_half
                lse_ref[res, rows, cols] = jnp.where(even_head, lse[ra], lse[rb])


def _attention_group(qkv, group):
    batch, dilation, sub_len, _ = qkv.shape
    tq = min(ATTN_ROWS_PER_STEP, sub_len)
    n_res = min(ATTN_ROWS_PER_STEP // tq, dilation)
    assert sub_len % tq == 0 and tq % Q_SUB == 0 and dilation % n_res == 0
    halo_per_tq = tq // N_SIDE
    n_halo = sub_len // N_SIDE
    slopes = tuple(float(np.float32(2.0 ** (-8.0 * (group * HEADS_PER_GROUP + h + 1) / N_ATTN_HEADS)))
                   for h in range(HEADS_PER_GROUP))

    def cur(part):
        return pl.BlockSpec((None, n_res, tq, GROUP_DIM), lambda b, r, i: (b, r, i, part))

    def prev(part):
        return pl.BlockSpec((None, n_res, N_SIDE, GROUP_DIM),
                            lambda b, r, i: (b, r, jnp.maximum(i * halo_per_tq - 1, 0), part))

    def nxt(part):
        return pl.BlockSpec((None, n_res, N_SIDE, GROUP_DIM),
                            lambda b, r, i: (b, r, jnp.minimum((i + 1) * halo_per_tq, n_halo - 1), part))

    out_spec = pl.BlockSpec((None, n_res, tq, GROUP_DIM), lambda b, r, i: (b, r, i, 0))
    out_shape = jax.ShapeDtypeStruct((batch, dilation, sub_len, GROUP_DIM), F32)
    return pl.pallas_call(
        functools.partial(_attn_kernel, dilation=dilation, slopes=slopes, sub_len=sub_len, tq=tq, n_res=n_res),
        grid=(batch, dilation // n_res, sub_len // tq),
        in_specs=[cur(0), prev(1), cur(1), nxt(1), prev(2), cur(2), nxt(2)],
        out_specs=[out_spec, out_spec],
        out_shape=[out_shape, out_shape],
        scratch_shapes=[pltpu.VMEM((n_res, tq + 2 * N_SIDE, GROUP_DIM), BF16),
                        pltpu.VMEM((n_res, tq + 2 * N_SIDE, GROUP_DIM), BF16),
                        pltpu.VMEM((HEADS_PER_GROUP * Q_SUB, GROUP_DIM), BF16),
                        pltpu.VMEM((4, HEADS_PER_GROUP * Q_SUB, K_WIN), F32)],
        compiler_params=_params(3),
        name=f"attn_d{dilation}",
    )(qkv, qkv, qkv, qkv, qkv, qkv, qkv)


def _pool_mean_minus_token(u_ref, up_ref, un_ref, ubuf, abuf, *, tm, seq):
    tiles_per_seq = seq // tm
    ti = pl.program_id(1)
    n = tm + 2 * POOL_HALO
    zero_halo = jnp.zeros((POOL_HALO, POOL_DIM), F32)
    ubuf[0:POOL_HALO] = jnp.where(ti == 0, zero_halo, up_ref[...])
    ubuf[POOL_HALO:POOL_HALO + tm] = u_ref[...]
    ubuf[POOL_HALO + tm:n] = jnp.where(ti == tiles_per_seq - 1, zero_halo, un_ref[...])
    ubuf[n:] = zero_halo
    abuf[n:] = zero_halo
    lo, hi = slice(0, LANES), slice(LANES, 2 * LANES)
    a2 = ubuf[0:n, lo] + ubuf[1:n + 1, lo]
    abuf[0:n, lo] = a2
    s2 = abuf[7:7 + tm, lo]
    ubuf[0:n, lo] = a2 + abuf[2:n + 2, lo]
    s4 = ubuf[6:6 + tm, lo]
    a2 = ubuf[0:n, hi] + ubuf[1:n + 1, hi]
    abuf[0:n, hi] = a2
    a4 = a2 + abuf[2:n + 2, hi]
    ubuf[0:n, hi] = a4
    a8 = a4 + ubuf[4:n + 4, hi]
    abuf[0:n, hi] = a8
    s8 = abuf[4:4 + tm, hi]
    s16 = a8[0:tm] + abuf[8:8 + tm, hi]
    first_group = lax.broadcasted_iota(jnp.int32, (1, LANES), 1) < POOL_GROUP_DIM
    t = ti * tm + lax.broadcasted_iota(jnp.int32, (tm, 1), 0)

    def mean(sum_a, sum_b, half_a, half_b):
        half = jnp.where(first_group, half_a, half_b)
        count = jnp.minimum(t + half, seq) - jnp.maximum(t - half, 0)
        return jnp.where(first_group, sum_a, sum_b) / count.astype(F32)

    return jnp.concatenate([mean(s2, s4, 1, 2), mean(s8, s16, 4, 8)], axis=-1) - u_ref[...]


def _token_order(blk_ref, buf, *, tm):
    dilation = blk_ref.shape[0]
    if dilation == 1:
        return blk_ref[0]
    n_slabs = GROUP_DIM // LANES
    part = tm // SPLIT_STRIDE
    if dilation == SPLIT_STRIDE:
        for r in range(dilation):
            for s in range(n_slabs):
                buf[0, s, pl.ds(r, part, stride=SPLIT_STRIDE), :] = blk_ref[r, :, s * LANES:(s + 1) * LANES]
        return jnp.concatenate([buf[0, s] for s in range(n_slabs)], axis=-1)
    assert dilation == SPLIT_STRIDE * SPLIT_STRIDE
    for r_lo in range(SPLIT_STRIDE):
        for r_hi in range(SPLIT_STRIDE):
            for s in range(n_slabs):
                buf[1, s, pl.ds(r_lo * part + r_hi, tm // dilation, stride=SPLIT_STRIDE), :] = (
                    blk_ref[r_lo + SPLIT_STRIDE * r_hi, :, s * LANES:(s + 1) * LANES])
    for r_lo in range(SPLIT_STRIDE):
        for s in range(n_slabs):
            buf[0, s, pl.ds(r_lo, part, stride=SPLIT_STRIDE), :] = buf[1, s, r_lo * part:(r_lo + 1) * part, :]
    return jnp.concatenate([buf[0, s] for s in range(n_slabs)], axis=-1)


def _mix_ffn_kernel(x_ref, u_ref, up_ref, un_ref, o1_ref, o2_ref, o3_ref, l1_ref, l2_ref, l3_ref,
                    wpool_ref, pscale_ref, wout_ref, gmix_ref, gpre_ref, wg_ref, wu_ref, wd_ref, gpost_ref,
                    out_ref, ubuf, abuf, ob2, ob3, lb2, lb3, acc_ref, *, tm, seq):
    y = _pool_mean_minus_token(u_ref, up_ref, un_ref, ubuf, abuf, tm=tm, seq=seq)
    a_pool = jnp.dot(y.astype(BF16), wpool_ref[...], preferred_element_type=F32) * pscale_ref[...]
    l1 = l1_ref[0]
    l2 = _token_order(l2_ref, lb2, tm=tm)
    l3 = _token_order(l3_ref, lb3, tm=tm)
    m = jnp.maximum(jnp.maximum(l1, l2), l3)
    e1, e2, e3 = jnp.exp(l1 - m), jnp.exp(l2 - m), jnp.exp(l3 - m)
    inv_z = 1.0 / (e1 + e2 + e3)
    o2 = _token_order(o2_ref, ob2, tm=tm)
    o3 = _token_order(o3_ref, ob3, tm=tm)
    cat = jnp.concatenate([a_pool, o1_ref[0] * (e1 * inv_z), o2 * (e2 * inv_z), o3 * (e3 * inv_z)],
                          axis=-1).astype(BF16)
    mix = jnp.dot(cat, wout_ref[...], preferred_element_type=F32)
    x2 = x_ref[...] + _rmsnorm(mix, gmix_ref[...])
    out_ref[...] = _ffn_tile(x2, gpre_ref, wg_ref, wu_ref, wd_ref, gpost_ref, acc_ref)


def _mix_ffn(x, u, attn, w_pool_bd, pool_scale, w_out, g_mix_post, g_pre, wg, wu, wd, g_post, *, tm):
    batch, seq, _ = x.shape
    n_halo = seq // POOL_HALO
    halo_per_tm = tm // POOL_HALO
    row = lambda width: pl.BlockSpec((None, tm, width), lambda b, i: (b, i, 0))
    prev = pl.BlockSpec((None, POOL_HALO, POOL_DIM), lambda b, i: (b, jnp.maximum(i * halo_per_tm - 1, 0), 0))
    nxt = pl.BlockSpec((None, POOL_HALO, POOL_DIM),
                       lambda b, i: (b, jnp.minimum((i + 1) * halo_per_tm, n_halo - 1), 0))
    by_residue = [pl.BlockSpec((None, d, tm // d, GROUP_DIM), lambda b, i: (b, 0, i, 0)) for _, d in ATTN_GROUPS]
    (o1, l1), (o2, l2), (o3, l3) = attn
    pool_buf = pltpu.VMEM((tm + 3 * POOL_HALO, POOL_DIM), F32)
    order_buf = pltpu.VMEM((2, GROUP_DIM // LANES, tm, LANES), F32)
    return pl.pallas_call(
        functools.partial(_mix_ffn_kernel, tm=tm, seq=seq),
        grid=(batch, seq // tm),
        in_specs=[row(D_MODEL), row(POOL_DIM), prev, nxt] + by_residue + by_residue +
                 [_const_spec((POOL_DIM, POOL_DIM)), _const_spec((1, POOL_DIM)),
                  _const_spec((D_MODEL, D_MODEL)), _const_spec((1, D_MODEL)), _const_spec((1, D_MODEL)),
                  _const_spec((D_MODEL, D_FF)), _const_spec((D_MODEL, D_FF)), _const_spec((D_FF, D_MODEL)),
                  _const_spec((1, D_MODEL))],
        out_specs=row(D_MODEL),
        out_shape=jax.ShapeDtypeStruct((batch, seq, D_MODEL), F32),
        scratch_shapes=[pool_buf, pool_buf, order_buf, order_buf, order_buf, order_buf,
                        pltpu.VMEM((tm, D_MODEL), F32)],
        compiler_params=_params(2),
        name="mix_ffn",
    )(x, u, u, u, o1, o2, o3, l1, l2, l3, w_pool_bd, pool_scale, w_out, g_mix_post, g_pre, wg, wu, wd, g_post)


def _ffn_weights(w_gate, w_up, w_down):
    return w_gate.astype(BF16), w_up.astype(BF16), w_down.astype(BF16)


def _qkv_by_group(w_qkv):
    d_model = w_qkv.shape[0]
    n_groups = len(ATTN_GROUPS)
    return w_qkv.reshape(d_model, 3, n_groups, GROUP_DIM).transpose(0, 2, 1, 3).reshape(d_model, D_QKV)


def _block_diag(w_lin):
    g, c, e = w_lin.shape
    eye = jnp.eye(g, dtype=w_lin.dtype)
    return (eye[:, None, :, None] * w_lin[:, :, None, :]).reshape(g * c, g * e)


def kernel(x, g_ffn1_pre, w1_gate, w1_up, w1_down, g_ffn1_post, g_mix_pre, w_in, w_pool_lin, pool_scale,
           w_out, g_mix_post, g_ffn2_pre, w2_gate, w2_up, w2_down, g_ffn2_post):
    batch, seq, _ = x.shape
    depth = g_ffn1_pre.shape[0]
    tm = 512
    assert seq % tm == 0 and tm % (ATTN_GROUPS[-1][1] * 16) == 0
    h = x
    for l in range(depth):
        w_in_bf = w_in[l].astype(BF16)
        x1, u, *qkv = _ffn_proj(h, g_ffn1_pre[l][None], *_ffn_weights(w1_gate[l], w1_up[l], w1_down[l]),
                                g_ffn1_post[l][None], g_mix_pre[l][None], w_in_bf[:, :POOL_DIM],
                                _qkv_by_group(w_in_bf[:, POOL_DIM:]), tm=tm)
        attn = [_attention_group(qkv[gi], gi) for gi in range(len(ATTN_GROUPS))]
        h = _mix_ffn(x1, u, attn, _block_diag(w_pool_lin[l]).astype(BF16), pool_scale[l][None],
                     w_out[l].astype(BF16), g_mix_post[l][None], g_ffn2_pre[l][None],
                     *_ffn_weights(w2_gate[l], w2_up[l], w2_down[l]), g_ffn2_post[l][None], tm=tm)
    return h
```

```python
import functools

import numpy as np
import jax
import jax.numpy as jnp
from jax import lax
from jax.experimental import pallas as pl
from jax.experimental.pallas import tpu as pltpu

D_MODEL = 1024
HEAD_DIM = 64
POOL_WINDOWS = (2, 4, 8, 16)
POOL_GROUP_DIM = 64
POOL_DIM = len(POOL_WINDOWS) * POOL_GROUP_DIM
ATTN_GROUPS = ((128, 1), (512, 4), (2048, 16))
HEADS_PER_GROUP = 4
N_ATTN_HEADS = HEADS_PER_GROUP * len(ATTN_GROUPS)
GROUP_DIM = HEADS_PER_GROUP * HEAD_DIM
ATTN_DIM = N_ATTN_HEADS * HEAD_DIM
D_QKV = 3 * ATTN_DIM
QKV_GROUP_DIM = 3 * GROUP_DIM
D_FF = 2752
RMS_EPS = 1e-6

LANES = 128
V7X_MXU_DIM = 256
V7X_VMEM_LIMIT_BYTES = 56 * 1024 * 1024
N_SIDE = 64
Q_SUB = 128
K_WIN = Q_SUB + 2 * N_SIDE
ATTN_ROWS_PER_STEP = 4096
SPLIT_STRIDE = 4
POOL_HALO = 8
MASKED = float("-inf")
NT_DIMS = (((1,), (1,)), ((), ()))

F32 = jnp.float32
BF16 = jnp.bfloat16


def _rmsnorm(x, g):
    ms = jnp.mean(x * x, axis=-1, keepdims=True)
    return x * lax.rsqrt(ms + RMS_EPS) * g


def _const_spec(shape):
    return pl.BlockSpec(shape, lambda *_: (0,) * len(shape), pipeline_mode=pl.Buffered(1))


def _params(n_axes):
    return pltpu.CompilerParams(dimension_semantics=("arbitrary",) * n_axes,
                                vmem_limit_bytes=V7X_VMEM_LIMIT_BYTES)


def _ffn_tile(x, gpre_ref, wg_ref, wu_ref, wd_ref, gpost_ref, acc_ref):
    xn = _rmsnorm(x, gpre_ref[...]).astype(BF16)
    for c0 in range(0, D_FF, V7X_MXU_DIM):
        c1 = min(c0 + V7X_MXU_DIM, D_FF)
        gate = lax.dot_general(xn, wg_ref[c0:c1, :], NT_DIMS, preferred_element_type=F32)
        up = lax.dot_general(xn, wu_ref[c0:c1, :], NT_DIMS, preferred_element_type=F32)
        act = (gate * jax.nn.sigmoid(gate) * up).astype(BF16)
        part = jnp.dot(act, wd_ref[c0:c1, :], preferred_element_type=F32)
        if c0 == 0:
            acc_ref[...] = part
        else:
            acc_ref[...] += part
    return x + 0.5 * _rmsnorm(acc_ref[...], gpost_ref[...])


def _ffn_proj_kernel(x_ref, gpre_ref, wg_ref, wu_ref, wd_ref, gpost_ref, gmix_ref, wpin_ref, wqkv_ref,
                     x1_ref, u_ref, q0_ref, q1_ref, q2_ref, acc_ref, zbuf, *, tm):
    x1 = _ffn_tile(x_ref[...], gpre_ref, wg_ref, wu_ref, wd_ref, gpost_ref, acc_ref)
    x1_ref[...] = x1
    h = _rmsnorm(x1, gmix_ref[...]).astype(BF16)
    u_ref[...] = jnp.dot(h, wpin_ref[...], preferred_element_type=F32)
    n_slabs = QKV_GROUP_DIM // LANES
    for gi, (out, (_, dilation)) in enumerate(zip((q0_ref, q1_ref, q2_ref), ATTN_GROUPS)):
        z = jnp.dot(h, wqkv_ref[:, gi * QKV_GROUP_DIM:(gi + 1) * QKV_GROUP_DIM], preferred_element_type=F32)
        if dilation == 1:
            out[0] = z.astype(BF16)
            continue
        for s in range(n_slabs):
            zbuf[0, s] = z[:, s * LANES:(s + 1) * LANES]
        part = tm // SPLIT_STRIDE
        if dilation == SPLIT_STRIDE:
            for r in range(dilation):
                for s in range(n_slabs):
                    out[r, :, s * LANES:(s + 1) * LANES] = (
                        zbuf[0, s, pl.ds(r, part, stride=SPLIT_STRIDE), :].astype(BF16))
            continue
        assert dilation == SPLIT_STRIDE * SPLIT_STRIDE
        for r_lo in range(SPLIT_STRIDE):
            for s in range(n_slabs):
                zbuf[1, s, r_lo * part:(r_lo + 1) * part, :] = zbuf[0, s, pl.ds(r_lo, part, stride=SPLIT_STRIDE), :]
        for r_lo in range(SPLIT_STRIDE):
            for r_hi in range(SPLIT_STRIDE):
                for s in range(n_slabs):
                    out[r_lo + SPLIT_STRIDE * r_hi, :, s * LANES:(s + 1) * LANES] = (
                        zbuf[1, s, pl.ds(r_lo * part + r_hi, tm // dilation, stride=SPLIT_STRIDE), :].astype(BF16))


def _ffn_proj(x, g_pre, wg, wu, wd, g_post, g_mix, w_pool_in, w_qkv, *, tm):
    batch, seq, _ = x.shape
    row = lambda width: pl.BlockSpec((None, tm, width), lambda b, i: (b, i, 0))
    qkv_specs = [pl.BlockSpec((None, d, tm // d, QKV_GROUP_DIM), lambda b, i: (b, 0, i, 0)) for _, d in ATTN_GROUPS]
    qkv_shapes = [jax.ShapeDtypeStruct((batch, d, seq // d, QKV_GROUP_DIM), BF16) for _, d in ATTN_GROUPS]
    return pl.pallas_call(
        functools.partial(_ffn_proj_kernel, tm=tm),
        grid=(batch, seq // tm),
        in_specs=[row(D_MODEL), _const_spec((1, D_MODEL)), _const_spec((D_FF, D_MODEL)),
                  _const_spec((D_FF, D_MODEL)), _const_spec((D_FF, D_MODEL)), _const_spec((1, D_MODEL)),
                  _const_spec((1, D_MODEL)), _const_spec((D_MODEL, POOL_DIM)), _const_spec((D_MODEL, D_QKV))],
        out_specs=[row(D_MODEL), row(POOL_DIM)] + qkv_specs,
        out_shape=[jax.ShapeDtypeStruct((batch, seq, D_MODEL), F32),
                   jax.ShapeDtypeStruct((batch, seq, POOL_DIM), F32)] + qkv_shapes,
        scratch_shapes=[pltpu.VMEM((tm, D_MODEL), F32),
                        pltpu.VMEM((len(ATTN_GROUPS) - 1, QKV_GROUP_DIM // LANES, tm, LANES), F32)],
        compiler_params=_params(2),
        name="ffn_proj",
    )(x, g_pre, wg, wu, wd, g_post, g_mix, w_pool_in, w_qkv)


def _attn_kernel(cur_ref, prev_ref, next_ref, out_ref,
                 kwin, vwin, qs_ref, bias_ref, *, dilation, slopes, sub_len, tq, n_res):
    first = (pl.program_id(0) == 0) & (pl.program_id(1) == 0) & (pl.program_id(2) == 0)

    @pl.when(first)
    def _():
        qi = lax.broadcasted_iota(jnp.int32, (Q_SUB, K_WIN), 0)
        kj = lax.broadcasted_iota(jnp.int32, (Q_SUB, K_WIN), 1)
        rel = jnp.abs(kj - N_SIDE - qi)
        dist = (rel * dilation).astype(F32)
        in_band = rel <= N_SIDE
        for variant in range(4):
            ok = in_band
            if variant & 1:
                ok = ok & (kj >= N_SIDE)
            if variant & 2:
                ok = ok & (kj < N_SIDE + Q_SUB)
            for h in range(HEADS_PER_GROUP):
                bias_ref[variant, h * Q_SUB:(h + 1) * Q_SUB, :] = jnp.where(ok, -slopes[h] * dist, MASKED)

    k_cols, v_cols = slice(GROUP_DIM, 2 * GROUP_DIM), slice(2 * GROUP_DIM, 3 * GROUP_DIM)
    for win, cols in ((kwin, k_cols), (vwin, v_cols)):
        win[:, 0:N_SIDE] = prev_ref[:, :, cols]
        win[:, N_SIDE:N_SIDE + tq] = cur_ref[:, :, cols]
        win[:, N_SIDE + tq:] = next_ref[:, :, cols]

    n_sub = tq // Q_SUB
    sub0 = pl.program_id(2) * n_sub
    last_sub = sub_len // Q_SUB - 1
    lane = lax.broadcasted_iota(jnp.int32, (1, GROUP_DIM), 1)
    head_of_lane = lane >> 6
    scale = HEAD_DIM ** -0.5
    even_head = lax.broadcasted_iota(jnp.int32, (1, LANES), 1) < HEAD_DIM

    for res in range(n_res):
        for j in range(n_sub):
            rows = slice(j * Q_SUB, (j + 1) * Q_SUB)
            q = cur_ref[res, rows, 0:GROUP_DIM]
            for h in range(HEADS_PER_GROUP):
                qs_ref[h * Q_SUB:(h + 1) * Q_SUB, :] = q * jnp.where(head_of_lane == h, scale, 0.0).astype(BF16)
            kw = kwin[res, j * Q_SUB:j * Q_SUB + K_WIN, :]
            vw = vwin[res, j * Q_SUB:j * Q_SUB + K_WIN, :]
            g = sub0 + j
            variant = (g == 0).astype(jnp.int32) + 2 * (g == last_sub).astype(jnp.int32)
            s = lax.dot_general(qs_ref[...], kw, (((1,), (1,)), ((), ())), preferred_element_type=F32)
            s = s + bias_ref[variant]
            m = jnp.max(s, axis=-1, keepdims=True)
            p = jnp.exp(s - m)
            den = jnp.sum(p, axis=-1, keepdims=True)
            pb = p.astype(BF16)
            p_wide = jnp.concatenate([pb[h * Q_SUB:(h + 1) * Q_SUB] for h in range(HEADS_PER_GROUP)], axis=-1)
            v_tall = jnp.concatenate([vw * (head_of_lane == h).astype(BF16) for h in range(HEADS_PER_GROUP)], axis=0)
            o = jnp.dot(p_wide, v_tall, preferred_element_type=F32)
            lse = m + jnp.log(den)
            for half in range(2):
                cols = slice(half * LANES, (half + 1) * LANES)
                ra = slice(2 * half * Q_SUB, (2 * half + 1) * Q_SUB)
                rb = slice((2 * half + 1) * Q_SUB, (2 * half + 2) * Q_SUB)
                den_half = jnp.where(even_head, den[ra], den[rb])
                out_ref[res, rows, cols] = o[:, cols] / den_half
                out_ref[res, rows, GROUP_DIM + half * LANES:GROUP_DIM + (half + 1) * LANES] = (
                    jnp.where(even_head, lse[ra], lse[rb]))


def _attention_group(qkv, group):
    batch, dilation, sub_len, _ = qkv.shape
    tq = min(ATTN_ROWS_PER_STEP, sub_len)
    n_res = min(ATTN_ROWS_PER_STEP // tq, dilation)
    assert sub_len % tq == 0 and tq % Q_SUB == 0 and dilation % n_res == 0
    halo_per_tq = tq // N_SIDE
    n_halo = sub_len // N_SIDE
    slopes = tuple(float(np.float32(2.0 ** (-8.0 * (group * HEADS_PER_GROUP + h + 1) / N_ATTN_HEADS)))
                   for h in range(HEADS_PER_GROUP))

    cur = pl.BlockSpec((None, n_res, tq, QKV_GROUP_DIM), lambda b, r, i: (b, r, i, 0))
    prev = pl.BlockSpec((None, n_res, N_SIDE, QKV_GROUP_DIM),
                        lambda b, r, i: (b, r, jnp.maximum(i * halo_per_tq - 1, 0), 0))
    nxt = pl.BlockSpec((None, n_res, N_SIDE, QKV_GROUP_DIM),
                       lambda b, r, i: (b, r, jnp.minimum((i + 1) * halo_per_tq, n_halo - 1), 0))
    return pl.pallas_call(
        functools.partial(_attn_kernel, dilation=dilation, slopes=slopes, sub_len=sub_len, tq=tq, n_res=n_res),
        grid=(batch, dilation // n_res, sub_len // tq),
        in_specs=[cur, prev, nxt],
        out_specs=pl.BlockSpec((None, n_res, tq, 2 * GROUP_DIM), lambda b, r, i: (b, r, i, 0)),
        out_shape=jax.ShapeDtypeStruct((batch, dilation, sub_len, 2 * GROUP_DIM), F32),
        scratch_shapes=[pltpu.VMEM((n_res, tq + 2 * N_SIDE, GROUP_DIM), BF16),
                        pltpu.VMEM((n_res, tq + 2 * N_SIDE, GROUP_DIM), BF16),
                        pltpu.VMEM((HEADS_PER_GROUP * Q_SUB, GROUP_DIM), BF16),
                        pltpu.VMEM((4, HEADS_PER_GROUP * Q_SUB, K_WIN), F32)],
        compiler_params=_params(3),
        name=f"attn_d{dilation}",
    )(qkv, qkv, qkv)


def _pool_mean_minus_token(u_ref, up_ref, un_ref, ubuf, abuf, *, tm, seq):
    tiles_per_seq = seq // tm
    ti = pl.program_id(1)
    n = tm + 2 * POOL_HALO
    zero_halo = jnp.zeros((POOL_HALO, POOL_DIM), F32)
    ubuf[0:POOL_HALO] = jnp.where(ti == 0, zero_halo, up_ref[...])
    ubuf[POOL_HALO:POOL_HALO + tm] = u_ref[...]
    ubuf[POOL_HALO + tm:n] = jnp.where(ti == tiles_per_seq - 1, zero_halo, un_ref[...])
    ubuf[n:] = zero_halo
    abuf[n:] = zero_halo
    lo, hi = slice(0, LANES), slice(LANES, 2 * LANES)
    a2 = ubuf[0:n, lo] + ubuf[1:n + 1, lo]
    abuf[0:n, lo] = a2
    s2 = abuf[7:7 + tm, lo]
    ubuf[0:n, lo] = a2 + abuf[2:n + 2, lo]
    s4 = ubuf[6:6 + tm, lo]
    a2 = ubuf[0:n, hi] + ubuf[1:n + 1, hi]
    abuf[0:n, hi] = a2
    a4 = a2 + abuf[2:n + 2, hi]
    ubuf[0:n, hi] = a4
    a8 = a4 + ubuf[4:n + 4, hi]
    abuf[0:n, hi] = a8
    s8 = abuf[4:4 + tm, hi]
    s16 = a8[0:tm] + abuf[8:8 + tm, hi]
    first_group = lax.broadcasted_iota(jnp.int32, (1, LANES), 1) < POOL_GROUP_DIM
    t = ti * tm + lax.broadcasted_iota(jnp.int32, (tm, 1), 0)

    def mean(sum_a, sum_b, half_a, half_b):
        half = jnp.where(first_group, half_a, half_b)
        count = jnp.minimum(t + half, seq) - jnp.maximum(t - half, 0)
        return jnp.where(first_group, sum_a, sum_b) / count.astype(F32)

    return jnp.concatenate([mean(s2, s4, 1, 2), mean(s8, s16, 4, 8)], axis=-1) - u_ref[...]


def _token_order(blk_ref, buf, *, tm):
    dilation = blk_ref.shape[0]
    if dilation == 1:
        return blk_ref[0]
    n_slabs = blk_ref.shape[-1] // LANES
    part = tm // SPLIT_STRIDE
    if dilation == SPLIT_STRIDE:
        for r in range(dilation):
            for s in range(n_slabs):
                buf[0, s, pl.ds(r, part, stride=SPLIT_STRIDE), :] = blk_ref[r, :, s * LANES:(s + 1) * LANES]
        return jnp.concatenate([buf[0, s] for s in range(n_slabs)], axis=-1)
    assert dilation == SPLIT_STRIDE * SPLIT_STRIDE
    for r_lo in range(SPLIT_STRIDE):
        for r_hi in range(SPLIT_STRIDE):
            for s in range(n_slabs):
                buf[1, s, pl.ds(r_lo * part + r_hi, tm // dilation, stride=SPLIT_STRIDE), :] = (
                    blk_ref[r_lo + SPLIT_STRIDE * r_hi, :, s * LANES:(s + 1) * LANES])
    for r_lo in range(SPLIT_STRIDE):
        for s in range(n_slabs):
            buf[0, s, pl.ds(r_lo, part, stride=SPLIT_STRIDE), :] = buf[1, s, r_lo * part:(r_lo + 1) * part, :]
    return jnp.concatenate([buf[0, s] for s in range(n_slabs)], axis=-1)


def _mix_ffn_kernel(x_ref, u_ref, up_ref, un_ref, a1_ref, a2_ref, a3_ref,
                    wpool_ref, pscale_ref, wout_ref, gmix_ref, gpre_ref, wg_ref, wu_ref, wd_ref, gpost_ref,
                    out_ref, ubuf, abuf, buf2, buf3, acc_ref, *, tm, seq):
    y = _pool_mean_minus_token(u_ref, up_ref, un_ref, ubuf, abuf, tm=tm, seq=seq)
    a_pool = jnp.dot(y.astype(BF16), wpool_ref[...], preferred_element_type=F32) * pscale_ref[...]
    groups = [a1_ref[0], _token_order(a2_ref, buf2, tm=tm), _token_order(a3_ref, buf3, tm=tm)]
    outs = [g[:, :GROUP_DIM] for g in groups]
    l1, l2, l3 = (g[:, GROUP_DIM:] for g in groups)
    m = jnp.maximum(jnp.maximum(l1, l2), l3)
    e1, e2, e3 = jnp.exp(l1 - m), jnp.exp(l2 - m), jnp.exp(l3 - m)
    inv_z = 1.0 / (e1 + e2 + e3)
    cat = jnp.concatenate([a_pool] + [o * (e * inv_z) for o, e in zip(outs, (e1, e2, e3))], axis=-1).astype(BF16)
    mix = jnp.dot(cat, wout_ref[...], preferred_element_type=F32)
    x2 = x_ref[...] + _rmsnorm(mix, gmix_ref[...])
    out_ref[...] = _ffn_tile(x2, gpre_ref, wg_ref, wu_ref, wd_ref, gpost_ref, acc_ref)


def _mix_ffn(x, u, attn, w_pool_bd, pool_scale, w_out, g_mix_post, g_pre, wg, wu, wd, g_post, *, tm):
    batch, seq, _ = x.shape
    n_halo = seq // POOL_HALO
    halo_per_tm = tm // POOL_HALO
    row = lambda width: pl.BlockSpec((None, tm, width), lambda b, i: (b, i, 0))
    prev = pl.BlockSpec((None, POOL_HALO, POOL_DIM), lambda b, i: (b, jnp.maximum(i * halo_per_tm - 1, 0), 0))
    nxt = pl.BlockSpec((None, POOL_HALO, POOL_DIM),
                       lambda b, i: (b, jnp.minimum((i + 1) * halo_per_tm, n_halo - 1), 0))
    by_residue = [pl.BlockSpec((None, d, tm // d, 2 * GROUP_DIM), lambda b, i: (b, 0, i, 0)) for _, d in ATTN_GROUPS]
    pool_buf = pltpu.VMEM((tm + 3 * POOL_HALO, POOL_DIM), F32)
    order_buf = pltpu.VMEM((2, 2 * GROUP_DIM // LANES, tm, LANES), F32)
    return pl.pallas_call(
        functools.partial(_mix_ffn_kernel, tm=tm, seq=seq),
        grid=(batch, seq // tm),
        in_specs=[row(D_MODEL), row(POOL_DIM), prev, nxt] + by_residue +
                 [_const_spec((POOL_DIM, POOL_DIM)), _const_spec((1, POOL_DIM)),
                  _const_spec((D_MODEL, D_MODEL)), _const_spec((1, D_MODEL)), _const_spec((1, D_MODEL)),
                  _const_spec((D_FF, D_MODEL)), _const_spec((D_FF, D_MODEL)), _const_spec((D_FF, D_MODEL)),
                  _const_spec((1, D_MODEL))],
        out_specs=row(D_MODEL),
        out_shape=jax.ShapeDtypeStruct((batch, seq, D_MODEL), F32),
        scratch_shapes=[pool_buf, pool_buf, order_buf, order_buf, pltpu.VMEM((tm, D_MODEL), F32)],
        compiler_params=_params(2),
        name="mix_ffn",
    )(x, u, u, u, *attn, w_pool_bd, pool_scale, w_out, g_mix_post, g_pre, wg, wu, wd, g_post)


def _ffn_weights(w_gate, w_up, w_down):
    return w_gate.T.astype(BF16), w_up.T.astype(BF16), w_down.astype(BF16)


def _qkv_by_group(w_qkv):
    d_model = w_qkv.shape[0]
    n_groups = len(ATTN_GROUPS)
    return w_qkv.reshape(d_model, 3, n_groups, GROUP_DIM).transpose(0, 2, 1, 3).reshape(d_model, D_QKV)


def _block_diag(w_lin):
    g, c, e = w_lin.shape
    eye = jnp.eye(g, dtype=w_lin.dtype)
    return (eye[:, None, :, None] * w_lin[:, :, None, :]).reshape(g * c, g * e)


def kernel(x, g_ffn1_pre, w1_gate, w1_up, w1_down, g_ffn1_post, g_mix_pre, w_in, w_pool_lin, pool_scale,
           w_out, g_mix_post, g_ffn2_pre, w2_gate, w2_up, w2_down, g_ffn2_post):
    batch, seq, _ = x.shape
    depth = g_ffn1_pre.shape[0]
    tm = 512
    assert seq % tm == 0 and tm % (ATTN_GROUPS[-1][1] * 16) == 0
    h = x
    for l in range(depth):
        w_in_bf = w_in[l].astype(BF16)
        x1, u, *qkv = _ffn_proj(h, g_ffn1_pre[l][None], *_ffn_weights(w1_gate[l], w1_up[l], w1_down[l]),
                                g_ffn1_post[l][None], g_mix_pre[l][None], w_in_bf[:, :POOL_DIM],
                                _qkv_by_group(w_in_bf[:, POOL_DIM:]), tm=tm)
        attn = [_attention_group(qkv[gi], gi) for gi in range(len(ATTN_GROUPS))]
        h = _mix_ffn(x1, u, attn, _block_diag(w_pool_lin[l]).astype(BF16), pool_scale[l][None],
                     w_out[l].astype(BF16), g_mix_post[l][None], g_ffn2_pre[l][None],
                     *_ffn_weights(w2_gate[l], w2_up[l], w2_down[l]), g_ffn2_post[l][None], tm=tm)
    return h
```

```python
import functools

import numpy as np
import jax
import jax.numpy as jnp
from jax import lax
from jax.experimental import pallas as pl
from jax.experimental.pallas import tpu as pltpu

D_MODEL = 1024
HEAD_DIM = 64
POOL_WINDOWS = (2, 4, 8, 16)
POOL_GROUP_DIM = 64
POOL_DIM = len(POOL_WINDOWS) * POOL_GROUP_DIM
ATTN_GROUPS = ((128, 1), (512, 4), (2048, 16))
HEADS_PER_GROUP = 4
N_ATTN_HEADS = HEADS_PER_GROUP * len(ATTN_GROUPS)
GROUP_DIM = HEADS_PER_GROUP * HEAD_DIM
ATTN_DIM = N_ATTN_HEADS * HEAD_DIM
D_QKV = 3 * ATTN_DIM
QKV_GROUP_DIM = 3 * GROUP_DIM
D_FF = 2752
RMS_EPS = 1e-6

LANES = 128
V7X_MXU_DIM = 256
V7X_VMEM_LIMIT_BYTES = 56 * 1024 * 1024
N_SIDE = 64
Q_SUB = 128
K_WIN = Q_SUB + 2 * N_SIDE
ATTN_ROWS_PER_STEP = 4096
SPLIT_STRIDE = 4
POOL_HALO = 8
MASKED = float("-inf")
NT_DIMS = (((1,), (1,)), ((), ()))

F32 = jnp.float32
BF16 = jnp.bfloat16


def _rmsnorm(x, g):
    ms = jnp.mean(x * x, axis=-1, keepdims=True)
    return x * lax.rsqrt(ms + RMS_EPS) * g


def _const_spec(shape):
    return pl.BlockSpec(shape, lambda *_: (0,) * len(shape), pipeline_mode=pl.Buffered(1))


def _params(n_axes):
    return pltpu.CompilerParams(dimension_semantics=("arbitrary",) * n_axes,
                                vmem_limit_bytes=V7X_VMEM_LIMIT_BYTES)


def _ffn_tile(x, gpre_ref, wg_ref, wu_ref, wd_ref, gpost_ref, acc_ref):
    xn = _rmsnorm(x, gpre_ref[...]).astype(BF16)
    for c0 in range(0, D_FF, V7X_MXU_DIM):
        c1 = min(c0 + V7X_MXU_DIM, D_FF)
        gate = lax.dot_general(xn, wg_ref[c0:c1, :], NT_DIMS, preferred_element_type=F32)
        up = lax.dot_general(xn, wu_ref[c0:c1, :], NT_DIMS, preferred_element_type=F32)
        act = (gate * jax.nn.sigmoid(gate) * up).astype(BF16)
        part = jnp.dot(act, wd_ref[c0:c1, :], preferred_element_type=F32)
        if c0 == 0:
            acc_ref[...] = part
        else:
            acc_ref[...] += part
    return x + 0.5 * _rmsnorm(acc_ref[...], gpost_ref[...])


def _ffn_proj_kernel(x_ref, gpre_ref, wg_ref, wu_ref, wd_ref, gpost_ref, gmix_ref, wpin_ref, wqkv_ref,
                     x1_ref, u_ref, q0_ref, q1_ref, q2_ref, acc_ref, zbuf, *, tm):
    x1 = _ffn_tile(x_ref[...], gpre_ref, wg_ref, wu_ref, wd_ref, gpost_ref, acc_ref)
    x1_ref[...] = x1
    h = _rmsnorm(x1, gmix_ref[...]).astype(BF16)
    u_ref[...] = jnp.dot(h, wpin_ref[...], preferred_element_type=F32)
    n_slabs = QKV_GROUP_DIM // LANES
    for gi, (out, (_, dilation)) in enumerate(zip((q0_ref, q1_ref, q2_ref), ATTN_GROUPS)):
        z = jnp.dot(h, wqkv_ref[:, gi * QKV_GROUP_DIM:(gi + 1) * QKV_GROUP_DIM], preferred_element_type=F32)
        if dilation == 1:
            out[0] = z.astype(BF16)
            continue
        for s in range(n_slabs):
            zbuf[0, s] = z[:, s * LANES:(s + 1) * LANES]
        part = tm // SPLIT_STRIDE
        if dilation == SPLIT_STRIDE:
            for r in range(dilation):
                for s in range(n_slabs):
                    out[r, :, s * LANES:(s + 1) * LANES] = (
                        zbuf[0, s, pl.ds(r, part, stride=SPLIT_STRIDE), :].astype(BF16))
            continue
        assert dilation == SPLIT_STRIDE * SPLIT_STRIDE
        for r_lo in range(SPLIT_STRIDE):
            for s in range(n_slabs):
                zbuf[1, s, r_lo * part:(r_lo + 1) * part, :] = zbuf[0, s, pl.ds(r_lo, part, stride=SPLIT_STRIDE), :]
        for r_lo in range(SPLIT_STRIDE):
            for r_hi in range(SPLIT_STRIDE):
                for s in range(n_slabs):
                    out[r_lo + SPLIT_STRIDE * r_hi, :, s * LANES:(s + 1) * LANES] = (
                        zbuf[1, s, pl.ds(r_lo * part + r_hi, tm // dilation, stride=SPLIT_STRIDE), :].astype(BF16))


def _ffn_proj(x, g_pre, wg, wu, wd, g_post, g_mix, w_pool_in, w_qkv, *, tm):
    batch, seq, _ = x.shape
    row = lambda width: pl.BlockSpec((None, tm, width), lambda b, i: (b, i, 0))
    qkv_specs = [pl.BlockSpec((None, d, tm // d, QKV_GROUP_DIM), lambda b, i: (b, 0, i, 0)) for _, d in ATTN_GROUPS]
    qkv_shapes = [jax.ShapeDtypeStruct((batch, d, seq // d, QKV_GROUP_DIM), BF16) for _, d in ATTN_GROUPS]
    return pl.pallas_call(
        functools.partial(_ffn_proj_kernel, tm=tm),
        grid=(batch, seq // tm),
        in_specs=[row(D_MODEL), _const_spec((1, D_MODEL)), _const_spec((D_FF, D_MODEL)),
                  _const_spec((D_FF, D_MODEL)), _const_spec((D_FF, D_MODEL)), _const_spec((1, D_MODEL)),
                  _const_spec((1, D_MODEL)), _const_spec((D_MODEL, POOL_DIM)), _const_spec((D_MODEL, D_QKV))],
        out_specs=[row(D_MODEL), row(POOL_DIM)] + qkv_specs,
        out_shape=[jax.ShapeDtypeStruct((batch, seq, D_MODEL), F32),
                   jax.ShapeDtypeStruct((batch, seq, POOL_DIM), F32)] + qkv_shapes,
        scratch_shapes=[pltpu.VMEM((tm, D_MODEL), F32),
                        pltpu.VMEM((len(ATTN_GROUPS) - 1, QKV_GROUP_DIM // LANES, tm, LANES), F32)],
        compiler_params=_params(2),
        name="ffn_proj",
    )(x, g_pre, wg, wu, wd, g_post, g_mix, w_pool_in, w_qkv)


def _attn_kernel(cur_ref, prev_ref, next_ref, out_ref,
                 kwin, vwin, qs_ref, bias_ref, *, dilation, slopes, sub_len, tq, n_res):
    first = (pl.program_id(0) == 0) & (pl.program_id(1) == 0) & (pl.program_id(2) == 0)

    @pl.when(first)
    def _():
        qi = lax.broadcasted_iota(jnp.int32, (Q_SUB, K_WIN), 0)
        kj = lax.broadcasted_iota(jnp.int32, (Q_SUB, K_WIN), 1)
        rel = jnp.abs(kj - N_SIDE - qi)
        dist = (rel * dilation).astype(F32)
        in_band = rel <= N_SIDE
        for variant in range(4):
            ok = in_band
            if variant & 1:
                ok = ok & (kj >= N_SIDE)
            if variant & 2:
                ok = ok & (kj < N_SIDE + Q_SUB)
            for h in range(HEADS_PER_GROUP):
                bias_ref[variant, h * Q_SUB:(h + 1) * Q_SUB, :] = jnp.where(ok, -slopes[h] * dist, MASKED)

    k_cols, v_cols = slice(GROUP_DIM, 2 * GROUP_DIM), slice(2 * GROUP_DIM, 3 * GROUP_DIM)
    for win, cols in ((kwin, k_cols), (vwin, v_cols)):
        win[:, 0:N_SIDE] = prev_ref[:, :, cols]
        win[:, N_SIDE:N_SIDE + tq] = cur_ref[:, :, cols]
        win[:, N_SIDE + tq:] = next_ref[:, :, cols]

    n_sub = tq // Q_SUB
    sub0 = pl.program_id(2) * n_sub
    last_sub = sub_len // Q_SUB - 1
    lane = lax.broadcasted_iota(jnp.int32, (1, GROUP_DIM), 1)
    head_of_lane = lane >> 6
    scale = HEAD_DIM ** -0.5
    even_head = lax.broadcasted_iota(jnp.int32, (1, LANES), 1) < HEAD_DIM

    for res in range(n_res):
        for j in range(n_sub):
            rows = slice(j * Q_SUB, (j + 1) * Q_SUB)
            q = cur_ref[res, rows, 0:GROUP_DIM]
            for h in range(HEADS_PER_GROUP):
                qs_ref[h * Q_SUB:(h + 1) * Q_SUB, :] = q * jnp.where(head_of_lane == h, scale, 0.0).astype(BF16)
            kw = kwin[res, j * Q_SUB:j * Q_SUB + K_WIN, :]
            vw = vwin[res, j * Q_SUB:j * Q_SUB + K_WIN, :]
            g = sub0 + j
            variant = (g == 0).astype(jnp.int32) + 2 * (g == last_sub).astype(jnp.int32)
            s = lax.dot_general(qs_ref[...], kw, (((1,), (1,)), ((), ())), preferred_element_type=F32)
            s = s + bias_ref[variant]
            m = jnp.max(s, axis=-1, keepdims=True)
            p = jnp.exp(s - m)
            den = jnp.sum(p, axis=-1, keepdims=True)
            pb = p.astype(BF16)
            p_wide = jnp.concatenate([pb[h * Q_SUB:(h + 1) * Q_SUB] for h in range(HEADS_PER_GROUP)], axis=-1)
            v_tall = jnp.concatenate([vw * (head_of_lane == h).astype(BF16) for h in range(HEADS_PER_GROUP)], axis=0)
            o = jnp.dot(p_wide, v_tall, preferred_element_type=F32)
            for half in range(2):
                cols = slice(half * LANES, (half + 1) * LANES)
                ra = slice(2 * half * Q_SUB, (2 * half + 1) * Q_SUB)
                rb = slice((2 * half + 1) * Q_SUB, (2 * half + 2) * Q_SUB)
                den_half = jnp.where(even_head, den[ra], den[rb])
                m_half = jnp.where(even_head, m[ra], m[rb])
                out_ref[res, rows, cols] = o[:, cols] / den_half
                out_ref[res, rows, GROUP_DIM + half * LANES:GROUP_DIM + (half + 1) * LANES] = (
                    m_half + jnp.log(den_half))


def _attention_group(qkv, group):
    batch, dilation, sub_len, _ = qkv.shape
    tq = min(ATTN_ROWS_PER_STEP, sub_len)
    n_res = min(ATTN_ROWS_PER_STEP // tq, dilation)
    assert sub_len % tq == 0 and tq % Q_SUB == 0 and dilation % n_res == 0
    halo_per_tq = tq // N_SIDE
    n_halo = sub_len // N_SIDE
    slopes = tuple(float(np.float32(2.0 ** (-8.0 * (group * HEADS_PER_GROUP + h + 1) / N_ATTN_HEADS)))
                   for h in range(HEADS_PER_GROUP))

    cur = pl.BlockSpec((None, n_res, tq, QKV_GROUP_DIM), lambda b, r, i: (b, r, i, 0))
    prev = pl.BlockSpec((None, n_res, N_SIDE, QKV_GROUP_DIM),
                        lambda b, r, i: (b, r, jnp.maximum(i * halo_per_tq - 1, 0), 0))
    nxt = pl.BlockSpec((None, n_res, N_SIDE, QKV_GROUP_DIM),
                       lambda b, r, i: (b, r, jnp.minimum((i + 1) * halo_per_tq, n_halo - 1), 0))
    return pl.pallas_call(
        functools.partial(_attn_kernel, dilation=dilation, slopes=slopes, sub_len=sub_len, tq=tq, n_res=n_res),
        grid=(batch, dilation // n_res, sub_len // tq),
        in_specs=[cur, prev, nxt],
        out_specs=pl.BlockSpec((None, n_res, tq, 2 * GROUP_DIM), lambda b, r, i: (b, r, i, 0)),
        out_shape=jax.ShapeDtypeStruct((batch, dilation, sub_len, 2 * GROUP_DIM), F32),
        scratch_shapes=[pltpu.VMEM((n_res, tq + 2 * N_SIDE, GROUP_DIM), BF16),
                        pltpu.VMEM((n_res, tq + 2 * N_SIDE, GROUP_DIM), BF16),
                        pltpu.VMEM((HEADS_PER_GROUP * Q_SUB, GROUP_DIM), BF16),
                        pltpu.VMEM((4, HEADS_PER_GROUP * Q_SUB, K_WIN), F32)],
        compiler_params=_params(3),
        name=f"attn_d{dilation}",
    )(qkv, qkv, qkv)


def _pool_mean_minus_token(u_ref, up_ref, un_ref, ubuf, abuf, *, tm, seq):
    tiles_per_seq = seq // tm
    ti = pl.program_id(1)
    n = tm + 2 * POOL_HALO
    zero_halo = jnp.zeros((POOL_HALO, POOL_DIM), F32)
    ubuf[0:POOL_HALO] = jnp.where(ti == 0, zero_halo, up_ref[...])
    ubuf[POOL_HALO:POOL_HALO + tm] = u_ref[...]
    ubuf[POOL_HALO + tm:n] = jnp.where(ti == tiles_per_seq - 1, zero_halo, un_ref[...])
    ubuf[n:] = zero_halo
    abuf[n:] = zero_halo
    lo, hi = slice(0, LANES), slice(LANES, 2 * LANES)
    a2 = ubuf[0:n, lo] + ubuf[1:n + 1, lo]
    abuf[0:n, lo] = a2
    s2 = abuf[7:7 + tm, lo]
    ubuf[0:n, lo] = a2 + abuf[2:n + 2, lo]
    s4 = ubuf[6:6 + tm, lo]
    a2 = ubuf[0:n, hi] + ubuf[1:n + 1, hi]
    abuf[0:n, hi] = a2
    a4 = a2 + abuf[2:n + 2, hi]
    ubuf[0:n, hi] = a4
    a8 = a4 + ubuf[4:n + 4, hi]
    abuf[0:n, hi] = a8
    s8 = abuf[4:4 + tm, hi]
    s16 = a8[0:tm] + abuf[8:8 + tm, hi]
    first_group = lax.broadcasted_iota(jnp.int32, (1, LANES), 1) < POOL_GROUP_DIM
    edge_row = lax.broadcasted_iota(jnp.int32, (POOL_HALO, 1), 0)

    def mean(sum_a, sum_b, half_a, half_b):
        half = jnp.where(first_group, half_a, half_b)
        win_sum = jnp.where(first_group, sum_a, sum_b)

        def edge(r0):
            t = ti * tm + r0 + edge_row
            count = jnp.minimum(t + half, seq) - jnp.maximum(t - half, 0)
            return win_sum[r0:r0 + POOL_HALO] / count.astype(F32)

        inv_window = jnp.where(first_group, 0.5 / half_a, 0.5 / half_b)
        return jnp.concatenate([edge(0), win_sum[POOL_HALO:tm - POOL_HALO] * inv_window, edge(tm - POOL_HALO)],
                               axis=0)

    return jnp.concatenate([mean(s2, s4, 1, 2), mean(s8, s16, 4, 8)], axis=-1) - u_ref[...]


def _token_order(blk_ref, buf, *, tm):
    dilation = blk_ref.shape[0]
    if dilation == 1:
        return blk_ref[0]
    n_slabs = blk_ref.shape[-1] // LANES
    part = tm // SPLIT_STRIDE
    if dilation == SPLIT_STRIDE:
        for r in range(dilation):
            for s in range(n_slabs):
                buf[0, s, pl.ds(r, part, stride=SPLIT_STRIDE), :] = blk_ref[r, :, s * LANES:(s + 1) * LANES]
        return jnp.concatenate([buf[0, s] for s in range(n_slabs)], axis=-1)
    assert dilation == SPLIT_STRIDE * SPLIT_STRIDE
    for r_lo in range(SPLIT_STRIDE):
        for r_hi in range(SPLIT_STRIDE):
            for s in range(n_slabs):
                buf[1, s, pl.ds(r_lo * part + r_hi, tm // dilation, stride=SPLIT_STRIDE), :] = (
                    blk_ref[r_lo + SPLIT_STRIDE * r_hi, :, s * LANES:(s + 1) * LANES])
    for r_lo in range(SPLIT_STRIDE):
        for s in range(n_slabs):
            buf[0, s, pl.ds(r_lo, part, stride=SPLIT_STRIDE), :] = buf[1, s, r_lo * part:(r_lo + 1) * part, :]
    return jnp.concatenate([buf[0, s] for s in range(n_slabs)], axis=-1)


def _mix_ffn_kernel(x_ref, u_ref, up_ref, un_ref, a1_ref, a2_ref, a3_ref,
                    wpool_ref, pscale_ref, wout_ref, gmix_ref, gpre_ref, wg_ref, wu_ref, wd_ref, gpost_ref,
                    out_ref, ubuf, abuf, buf2, buf3, acc_ref, *, tm, seq):
    y = _pool_mean_minus_token(u_ref, up_ref, un_ref, ubuf, abuf, tm=tm, seq=seq)
    a_pool = jnp.dot(y.astype(BF16), wpool_ref[...], preferred_element_type=F32) * pscale_ref[...]
    groups = [a1_ref[0], _token_order(a2_ref, buf2, tm=tm), _token_order(a3_ref, buf3, tm=tm)]
    outs = [g[:, :GROUP_DIM] for g in groups]
    l1, l2, l3 = (g[:, GROUP_DIM:] for g in groups)
    m = jnp.maximum(jnp.maximum(l1, l2), l3)
    e1, e2, e3 = jnp.exp(l1 - m), jnp.exp(l2 - m), jnp.exp(l3 - m)
    inv_z = 1.0 / (e1 + e2 + e3)
    cat = jnp.concatenate([a_pool] + [o * (e * inv_z) for o, e in zip(outs, (e1, e2, e3))], axis=-1).astype(BF16)
    mix = jnp.dot(cat, wout_ref[...], preferred_element_type=F32)
    x2 = x_ref[...] + _rmsnorm(mix, gmix_ref[...])
    out_ref[...] = _ffn_tile(x2, gpre_ref, wg_ref, wu_ref, wd_ref, gpost_ref, acc_ref)


def _mix_ffn(x, u, attn, w_pool_bd, pool_scale, w_out, g_mix_post, g_pre, wg, wu, wd, g_post, *, tm):
    batch, seq, _ = x.shape
    n_halo = seq // POOL_HALO
    halo_per_tm = tm // POOL_HALO
    row = lambda width: pl.BlockSpec((None, tm, width), lambda b, i: (b, i, 0))
    prev = pl.BlockSpec((None, POOL_HALO, POOL_DIM), lambda b, i: (b, jnp.maximum(i * halo_per_tm - 1, 0), 0))
    nxt = pl.BlockSpec((None, POOL_HALO, POOL_DIM),
                       lambda b, i: (b, jnp.minimum((i + 1) * halo_per_tm, n_halo - 1), 0))
    by_residue = [pl.BlockSpec((None, d, tm // d, 2 * GROUP_DIM), lambda b, i: (b, 0, i, 0)) for _, d in ATTN_GROUPS]
    pool_buf = pltpu.VMEM((tm + 3 * POOL_HALO, POOL_DIM), F32)
    order_buf = pltpu.VMEM((2, 2 * GROUP_DIM // LANES, tm, LANES), F32)
    return pl.pallas_call(
        functools.partial(_mix_ffn_kernel, tm=tm, seq=seq),
        grid=(batch, seq // tm),
        in_specs=[row(D_MODEL), row(POOL_DIM), prev, nxt] + by_residue +
                 [_const_spec((POOL_DIM, POOL_DIM)), _const_spec((1, POOL_DIM)),
                  _const_spec((D_MODEL, D_MODEL)), _const_spec((1, D_MODEL)), _const_spec((1, D_MODEL)),
                  _const_spec((D_FF, D_MODEL)), _const_spec((D_FF, D_MODEL)), _const_spec((D_FF, D_MODEL)),
                  _const_spec((1, D_MODEL))],
        out_specs=row(D_MODEL),
        out_shape=jax.ShapeDtypeStruct((batch, seq, D_MODEL), F32),
        scratch_shapes=[pool_buf, pool_buf, order_buf, order_buf, pltpu.VMEM((tm, D_MODEL), F32)],
        compiler_params=_params(2),
        name="mix_ffn",
    )(x, u, u, u, *attn, w_pool_bd, pool_scale, w_out, g_mix_post, g_pre, wg, wu, wd, g_post)


def _ffn_weights(w_gate, w_up, w_down):
    return w_gate.T.astype(BF16), w_up.T.astype(BF16), w_down.astype(BF16)


def _qkv_by_group(w_qkv):
    d_model = w_qkv.shape[0]
    n_groups = len(ATTN_GROUPS)
    return w_qkv.reshape(d_model, 3, n_groups, GROUP_DIM).transpose(0, 2, 1, 3).reshape(d_model, D_QKV)


def _block_diag(w_lin):
    g, c, e = w_lin.shape
    eye = jnp.eye(g, dtype=w_lin.dtype)
    return (eye[:, None, :, None] * w_lin[:, :, None, :]).reshape(g * c, g * e)


def kernel(x, g_ffn1_pre, w1_gate, w1_up, w1_down, g_ffn1_post, g_mix_pre, w_in, w_pool_lin, pool_scale,
           w_out, g_mix_post, g_ffn2_pre, w2_gate, w2_up, w2_down, g_ffn2_post):
    batch, seq, _ = x.shape
    depth = g_ffn1_pre.shape[0]
    tm = 512
    assert seq % tm == 0 and tm % (ATTN_GROUPS[-1][1] * 16) == 0
    h = x
    for l in range(depth):
        w_in_bf = w_in[l].astype(BF16)
        x1, u, *qkv = _ffn_proj(h, g_ffn1_pre[l][None], *_ffn_weights(w1_gate[l], w1_up[l], w1_down[l]),
                                g_ffn1_post[l][None], g_mix_pre[l][None], w_in_bf[:, :POOL_DIM],
                                _qkv_by_group(w_in_bf[:, POOL_DIM:]), tm=tm)
        attn = [_attention_group(qkv[gi], gi) for gi in range(len(ATTN_GROUPS))]
        h = _mix_ffn(x1, u, attn, _block_diag(w_pool_lin[l]).astype(BF16), pool_scale[l][None],
                     w_out[l].astype(BF16), g_mix_post[l][None], g_ffn2_pre[l][None],
                     *_ffn_weights(w2_gate[l], w2_up[l], w2_down[l]), g_ffn2_post[l][None], tm=tm)
    return h
```

```python
import functools

import numpy as np
import jax
import jax.numpy as jnp
from jax import lax
from jax.experimental import pallas as pl
from jax.experimental.pallas import tpu as pltpu

D_MODEL = 1024
HEAD_DIM = 64
POOL_WINDOWS = (2, 4, 8, 16)
POOL_GROUP_DIM = 64
POOL_DIM = len(POOL_WINDOWS) * POOL_GROUP_DIM
ATTN_GROUPS = ((128, 1), (512, 4), (2048, 16))
HEADS_PER_GROUP = 4
N_ATTN_HEADS = HEADS_PER_GROUP * len(ATTN_GROUPS)
GROUP_DIM = HEADS_PER_GROUP * HEAD_DIM
ATTN_DIM = N_ATTN_HEADS * HEAD_DIM
D_QKV = 3 * ATTN_DIM
QKV_GROUP_DIM = 3 * GROUP_DIM
D_FF = 2752
RMS_EPS = 1e-6

LANES = 128
V7X_MXU_DIM = 256
V7X_VMEM_LIMIT_BYTES = 56 * 1024 * 1024
N_SIDE = 64
Q_SUB = 128
K_WIN = Q_SUB + 2 * N_SIDE
ATTN_ROWS_PER_STEP = 4096
SPLIT_STRIDE = 4
POOL_HALO = 8
MASKED = float("-inf")
LOG2_E = 1.4426950408889634
Q_SCALE = HEAD_DIM ** -0.5 * LOG2_E
NT_DIMS = (((1,), (1,)), ((), ()))

F32 = jnp.float32
BF16 = jnp.bfloat16


def _rmsnorm(x, g):
    ms = jnp.mean(x * x, axis=-1, keepdims=True)
    return x * lax.rsqrt(ms + RMS_EPS) * g


def _const_spec(shape):
    return pl.BlockSpec(shape, lambda *_: (0,) * len(shape), pipeline_mode=pl.Buffered(1))


def _params(n_axes):
    return pltpu.CompilerParams(dimension_semantics=("arbitrary",) * n_axes,
                                vmem_limit_bytes=V7X_VMEM_LIMIT_BYTES)


def _ffn_tile(x, gpre_ref, wg_ref, wu_ref, wd_ref, gpost_ref, acc_ref):
    xn = _rmsnorm(x, gpre_ref[...]).astype(BF16)
    for c0 in range(0, D_FF, V7X_MXU_DIM):
        c1 = min(c0 + V7X_MXU_DIM, D_FF)
        gate = lax.dot_general(xn, wg_ref[c0:c1, :], NT_DIMS, preferred_element_type=F32)
        up = lax.dot_general(xn, wu_ref[c0:c1, :], NT_DIMS, preferred_element_type=F32)
        act = (gate * jax.nn.sigmoid(gate) * up).astype(BF16)
        part = jnp.dot(act, wd_ref[c0:c1, :], preferred_element_type=F32)
        if c0 == 0:
            acc_ref[...] = part
        else:
            acc_ref[...] += part
    return x + 0.5 * _rmsnorm(acc_ref[...], gpost_ref[...])


def _ffn_proj_kernel(x_ref, gpre_ref, wg_ref, wu_ref, wd_ref, gpost_ref, gmix_ref, wpin_ref, wqkv_ref,
                     x1_ref, u_ref, q0_ref, q1_ref, q2_ref, acc_ref, zbuf, *, tm):
    x1 = _ffn_tile(x_ref[...], gpre_ref, wg_ref, wu_ref, wd_ref, gpost_ref, acc_ref)
    x1_ref[...] = x1
    h = _rmsnorm(x1, gmix_ref[...]).astype(BF16)
    u_ref[...] = jnp.dot(h, wpin_ref[...], preferred_element_type=F32)
    n_slabs = QKV_GROUP_DIM // LANES
    for gi, (out, (_, dilation)) in enumerate(zip((q0_ref, q1_ref, q2_ref), ATTN_GROUPS)):
        z = jnp.dot(h, wqkv_ref[:, gi * QKV_GROUP_DIM:(gi + 1) * QKV_GROUP_DIM], preferred_element_type=F32)
        z = jnp.concatenate([z[:, :GROUP_DIM] * Q_SCALE, z[:, GROUP_DIM:]], axis=-1)
        if dilation == 1:
            out[0] = z.astype(BF16)
            continue
        for s in range(n_slabs):
            zbuf[0, s] = z[:, s * LANES:(s + 1) * LANES]
        part = tm // SPLIT_STRIDE
        if dilation == SPLIT_STRIDE:
            for r in range(dilation):
                for s in range(n_slabs):
                    out[r, :, s * LANES:(s + 1) * LANES] = (
                        zbuf[0, s, pl.ds(r, part, stride=SPLIT_STRIDE), :].astype(BF16))
            continue
        assert dilation == SPLIT_STRIDE * SPLIT_STRIDE
        for r_lo in range(SPLIT_STRIDE):
            for s in range(n_slabs):
                zbuf[1, s, r_lo * part:(r_lo + 1) * part, :] = zbuf[0, s, pl.ds(r_lo, part, stride=SPLIT_STRIDE), :]
        for r_lo in range(SPLIT_STRIDE):
            for r_hi in range(SPLIT_STRIDE):
                for s in range(n_slabs):
                    out[r_lo + SPLIT_STRIDE * r_hi, :, s * LANES:(s + 1) * LANES] = (
                        zbuf[1, s, pl.ds(r_lo * part + r_hi, tm // dilation, stride=SPLIT_STRIDE), :].astype(BF16))


def _ffn_proj(x, g_pre, wg, wu, wd, g_post, g_mix, w_pool_in, w_qkv, *, tm):
    batch, seq, _ = x.shape
    row = lambda width: pl.BlockSpec((None, tm, width), lambda b, i: (b, i, 0))
    qkv_specs = [pl.BlockSpec((None, d, tm // d, QKV_GROUP_DIM), lambda b, i: (b, 0, i, 0)) for _, d in ATTN_GROUPS]
    qkv_shapes = [jax.ShapeDtypeStruct((batch, d, seq // d, QKV_GROUP_DIM), BF16) for _, d in ATTN_GROUPS]
    return pl.pallas_call(
        functools.partial(_ffn_proj_kernel, tm=tm),
        grid=(batch, seq // tm),
        in_specs=[row(D_MODEL), _const_spec((1, D_MODEL)), _const_spec((D_FF, D_MODEL)),
                  _const_spec((D_FF, D_MODEL)), _const_spec((D_FF, D_MODEL)), _const_spec((1, D_MODEL)),
                  _const_spec((1, D_MODEL)), _const_spec((D_MODEL, POOL_DIM)), _const_spec((D_MODEL, D_QKV))],
        out_specs=[row(D_MODEL), row(POOL_DIM)] + qkv_specs,
        out_shape=[jax.ShapeDtypeStruct((batch, seq, D_MODEL), F32),
                   jax.ShapeDtypeStruct((batch, seq, POOL_DIM), F32)] + qkv_shapes,
        scratch_shapes=[pltpu.VMEM((tm, D_MODEL), F32),
                        pltpu.VMEM((len(ATTN_GROUPS) - 1, QKV_GROUP_DIM // LANES, tm, LANES), F32)],
        compiler_params=_params(2),
        name="ffn_proj",
    )(x, g_pre, wg, wu, wd, g_post, g_mix, w_pool_in, w_qkv)


def _attn_kernel(cur_ref, prev_ref, next_ref, out_ref,
                 kwin, vwin, qs_ref, bias_ref, *, dilation, slopes, sub_len, tq, n_res):
    first = (pl.program_id(0) == 0) & (pl.program_id(1) == 0) & (pl.program_id(2) == 0)

    @pl.when(first)
    def _():
        qi = lax.broadcasted_iota(jnp.int32, (Q_SUB, K_WIN), 0)
        kj = lax.broadcasted_iota(jnp.int32, (Q_SUB, K_WIN), 1)
        rel = jnp.abs(kj - N_SIDE - qi)
        dist = (rel * dilation).astype(F32)
        in_band = rel <= N_SIDE
        for variant in range(4):
            ok = in_band
            if variant & 1:
                ok = ok & (kj >= N_SIDE)
            if variant & 2:
                ok = ok & (kj < N_SIDE + Q_SUB)
            for h in range(HEADS_PER_GROUP):
                bias_ref[variant, h * Q_SUB:(h + 1) * Q_SUB, :] = jnp.where(ok, -slopes[h] * dist * LOG2_E, MASKED)

    k_cols, v_cols = slice(GROUP_DIM, 2 * GROUP_DIM), slice(2 * GROUP_DIM, 3 * GROUP_DIM)
    for win, cols in ((kwin, k_cols), (vwin, v_cols)):
        win[:, 0:N_SIDE] = prev_ref[:, :, cols]
        win[:, N_SIDE:N_SIDE + tq] = cur_ref[:, :, cols]
        win[:, N_SIDE + tq:] = next_ref[:, :, cols]

    n_sub = tq // Q_SUB
    sub0 = pl.program_id(2) * n_sub
    last_sub = sub_len // Q_SUB - 1
    lane = lax.broadcasted_iota(jnp.int32, (1, GROUP_DIM), 1)
    head_of_lane = lane >> 6
    even_head = lax.broadcasted_iota(jnp.int32, (1, LANES), 1) < HEAD_DIM

    for res in range(n_res):
        for j in range(n_sub):
            rows = slice(j * Q_SUB, (j + 1) * Q_SUB)
            q = cur_ref[res, rows, 0:GROUP_DIM]
            for h in range(HEADS_PER_GROUP):
                qs_ref[h * Q_SUB:(h + 1) * Q_SUB, :] = q * (head_of_lane == h).astype(BF16)
            kw = kwin[res, j * Q_SUB:j * Q_SUB + K_WIN, :]
            vw = vwin[res, j * Q_SUB:j * Q_SUB + K_WIN, :]
            g = sub0 + j
            variant = (g == 0).astype(jnp.int32) + 2 * (g == last_sub).astype(jnp.int32)
            s = lax.dot_general(qs_ref[...], kw, (((1,), (1,)), ((), ())), preferred_element_type=F32)
            s = s + bias_ref[variant]
            m = jnp.max(s, axis=-1, keepdims=True)
            p = jnp.exp2(s - m)
            den = jnp.sum(p, axis=-1, keepdims=True)
            pb = p.astype(BF16)
            p_wide = jnp.concatenate([pb[h * Q_SUB:(h + 1) * Q_SUB] for h in range(HEADS_PER_GROUP)], axis=-1)
            v_tall = jnp.concatenate([vw * (head_of_lane == h).astype(BF16) for h in range(HEADS_PER_GROUP)], axis=0)
            o = jnp.dot(p_wide, v_tall, preferred_element_type=F32)
            for half in range(2):
                cols = slice(half * LANES, (half + 1) * LANES)
                ra = slice(2 * half * Q_SUB, (2 * half + 1) * Q_SUB)
                rb = slice((2 * half + 1) * Q_SUB, (2 * half + 2) * Q_SUB)
                den_half = jnp.where(even_head, den[ra], den[rb])
                m_half = jnp.where(even_head, m[ra], m[rb])
                out_ref[res, rows, cols] = o[:, cols] / den_half
                out_ref[res, rows, GROUP_DIM + half * LANES:GROUP_DIM + (half + 1) * LANES] = (
                    m_half + jnp.log2(den_half))


def _attention_group(qkv, group):
    batch, dilation, sub_len, _ = qkv.shape
    tq = min(ATTN_ROWS_PER_STEP, sub_len)
    n_res = min(ATTN_ROWS_PER_STEP // tq, dilation)
    assert sub_len % tq == 0 and tq % Q_SUB == 0 and dilation % n_res == 0
    halo_per_tq = tq // N_SIDE
    n_halo = sub_len // N_SIDE
    slopes = tuple(float(np.float32(2.0 ** (-8.0 * (group * HEADS_PER_GROUP + h + 1) / N_ATTN_HEADS)))
                   for h in range(HEADS_PER_GROUP))

    cur = pl.BlockSpec((None, n_res, tq, QKV_GROUP_DIM), lambda b, r, i: (b, r, i, 0))
    prev = pl.BlockSpec((None, n_res, N_SIDE, QKV_GROUP_DIM),
                        lambda b, r, i: (b, r, jnp.maximum(i * halo_per_tq - 1, 0), 0))
    nxt = pl.BlockSpec((None, n_res, N_SIDE, QKV_GROUP_DIM),
                       lambda b, r, i: (b, r, jnp.minimum((i + 1) * halo_per_tq, n_halo - 1), 0))
    return pl.pallas_call(
        functools.partial(_attn_kernel, dilation=dilation, slopes=slopes, sub_len=sub_len, tq=tq, n_res=n_res),
        grid=(batch, dilation // n_res, sub_len // tq),
        in_specs=[cur, prev, nxt],
        out_specs=pl.BlockSpec((None, n_res, tq, 2 * GROUP_DIM), lambda b, r, i: (b, r, i, 0)),
        out_shape=jax.ShapeDtypeStruct((batch, dilation, sub_len, 2 * GROUP_DIM), F32),
        scratch_shapes=[pltpu.VMEM((n_res, tq + 2 * N_SIDE, GROUP_DIM), BF16),
                        pltpu.VMEM((n_res, tq + 2 * N_SIDE, GROUP_DIM), BF16),
                        pltpu.VMEM((HEADS_PER_GROUP * Q_SUB, GROUP_DIM), BF16),
                        pltpu.VMEM((4, HEADS_PER_GROUP * Q_SUB, K_WIN), F32)],
        compiler_params=_params(3),
        name=f"attn_d{dilation}",
    )(qkv, qkv, qkv)


def _pool_mean_minus_token(u_ref, up_ref, un_ref, ubuf, abuf, *, tm, seq):
    tiles_per_seq = seq // tm
    ti = pl.program_id(1)
    n = tm + 2 * POOL_HALO
    zero_halo = jnp.zeros((POOL_HALO, POOL_DIM), F32)
    ubuf[0:POOL_HALO] = jnp.where(ti == 0, zero_halo, up_ref[...])
    ubuf[POOL_HALO:POOL_HALO + tm] = u_ref[...]
    ubuf[POOL_HALO + tm:n] = jnp.where(ti == tiles_per_seq - 1, zero_halo, un_ref[...])
    ubuf[n:] = zero_halo
    abuf[n:] = zero_halo
    lo, hi = slice(0, LANES), slice(LANES, 2 * LANES)
    a2 = ubuf[0:n, lo] + ubuf[1:n + 1, lo]
    abuf[0:n, lo] = a2
    s2 = abuf[7:7 + tm, lo]
    ubuf[0:n, lo] = a2 + abuf[2:n + 2, lo]
    s4 = ubuf[6:6 + tm, lo]
    a2 = ubuf[0:n, hi] + ubuf[1:n + 1, hi]
    abuf[0:n, hi] = a2
    a4 = a2 + abuf[2:n + 2, hi]
    ubuf[0:n, hi] = a4
    a8 = a4 + ubuf[4:n + 4, hi]
    abuf[0:n, hi] = a8
    s8 = abuf[4:4 + tm, hi]
    s16 = a8[0:tm] + abuf[8:8 + tm, hi]
    first_group = lax.broadcasted_iota(jnp.int32, (1, LANES), 1) < POOL_GROUP_DIM
    edge_row = lax.broadcasted_iota(jnp.int32, (POOL_HALO, 1), 0)

    def mean(sum_a, sum_b, half_a, half_b):
        half = jnp.where(first_group, half_a, half_b)
        win_sum = jnp.where(first_group, sum_a, sum_b)

        def edge(r0):
            t = ti * tm + r0 + edge_row
            count = jnp.minimum(t + half, seq) - jnp.maximum(t - half, 0)
            return win_sum[r0:r0 + POOL_HALO] / count.astype(F32)

        inv_window = jnp.where(first_group, 0.5 / half_a, 0.5 / half_b)
        return jnp.concatenate([edge(0), win_sum[POOL_HALO:tm - POOL_HALO] * inv_window, edge(tm - POOL_HALO)],
                               axis=0)

    return jnp.concatenate([mean(s2, s4, 1, 2), mean(s8, s16, 4, 8)], axis=-1) - u_ref[...]


def _token_order(blk_ref, buf, *, tm):
    dilation = blk_ref.shape[0]
    if dilation == 1:
        return blk_ref[0]
    n_slabs = blk_ref.shape[-1] // LANES
    part = tm // SPLIT_STRIDE
    if dilation == SPLIT_STRIDE:
        for r in range(dilation):
            for s in range(n_slabs):
                buf[0, s, pl.ds(r, part, stride=SPLIT_STRIDE), :] = blk_ref[r, :, s * LANES:(s + 1) * LANES]
        return jnp.concatenate([buf[0, s] for s in range(n_slabs)], axis=-1)
    assert dilation == SPLIT_STRIDE * SPLIT_STRIDE
    for r_lo in range(SPLIT_STRIDE):
        for r_hi in range(SPLIT_STRIDE):
            for s in range(n_slabs):
                buf[1, s, pl.ds(r_lo * part + r_hi, tm // dilation, stride=SPLIT_STRIDE), :] = (
                    blk_ref[r_lo + SPLIT_STRIDE * r_hi, :, s * LANES:(s + 1) * LANES])
    for r_lo in range(SPLIT_STRIDE):
        for s in range(n_slabs):
            buf[0, s, pl.ds(r_lo, part, stride=SPLIT_STRIDE), :] = buf[1, s, r_lo * part:(r_lo + 1) * part, :]
    return jnp.concatenate([buf[0, s] for s in range(n_slabs)], axis=-1)


def _mix_ffn_kernel(x_ref, u_ref, up_ref, un_ref, a1_ref, a2_ref, a3_ref,
                    wpool_ref, pscale_ref, wout_ref, gmix_ref, gpre_ref, wg_ref, wu_ref, wd_ref, gpost_ref,
                    out_ref, ubuf, abuf, buf2, buf3, acc_ref, *, tm, seq):
    y = _pool_mean_minus_token(u_ref, up_ref, un_ref, ubuf, abuf, tm=tm, seq=seq)
    a_pool = jnp.dot(y.astype(BF16), wpool_ref[...], preferred_element_type=F32) * pscale_ref[...]
    groups = [a1_ref[0], _token_order(a2_ref, buf2, tm=tm), _token_order(a3_ref, buf3, tm=tm)]
    outs = [g[:, :GROUP_DIM] for g in groups]
    l1, l2, l3 = (g[:, GROUP_DIM:] for g in groups)
    m = jnp.maximum(jnp.maximum(l1, l2), l3)
    e1, e2, e3 = jnp.exp2(l1 - m), jnp.exp2(l2 - m), jnp.exp2(l3 - m)
    inv_z = 1.0 / (e1 + e2 + e3)
    cat = jnp.concatenate([a_pool] + [o * (e * inv_z) for o, e in zip(outs, (e1, e2, e3))], axis=-1).astype(BF16)
    mix = jnp.dot(cat, wout_ref[...], preferred_element_type=F32)
    x2 = x_ref[...] + _rmsnorm(mix, gmix_ref[...])
    out_ref[...] = _ffn_tile(x2, gpre_ref, wg_ref, wu_ref, wd_ref, gpost_ref, acc_ref)


def _mix_ffn(x, u, attn, w_pool_bd, pool_scale, w_out, g_mix_post, g_pre, wg, wu, wd, g_post, *, tm):
    batch, seq, _ = x.shape
    n_halo = seq // POOL_HALO
    halo_per_tm = tm // POOL_HALO
    row = lambda width: pl.BlockSpec((None, tm, width), lambda b, i: (b, i, 0))
    prev = pl.BlockSpec((None, POOL_HALO, POOL_DIM), lambda b, i: (b, jnp.maximum(i * halo_per_tm - 1, 0), 0))
    nxt = pl.BlockSpec((None, POOL_HALO, POOL_DIM),
                       lambda b, i: (b, jnp.minimum((i + 1) * halo_per_tm, n_halo - 1), 0))
    by_residue = [pl.BlockSpec((None, d, tm // d, 2 * GROUP_DIM), lambda b, i: (b, 0, i, 0)) for _, d in ATTN_GROUPS]
    pool_buf = pltpu.VMEM((tm + 3 * POOL_HALO, POOL_DIM), F32)
    order_buf = pltpu.VMEM((2, 2 * GROUP_DIM // LANES, tm, LANES), F32)
    return pl.pallas_call(
        functools.partial(_mix_ffn_kernel, tm=tm, seq=seq),
        grid=(batch, seq // tm),
        in_specs=[row(D_MODEL), row(POOL_DIM), prev, nxt] + by_residue +
                 [_const_spec((POOL_DIM, POOL_DIM)), _const_spec((1, POOL_DIM)),
                  _const_spec((D_MODEL, D_MODEL)), _const_spec((1, D_MODEL)), _const_spec((1, D_MODEL)),
                  _const_spec((D_FF, D_MODEL)), _const_spec((D_FF, D_MODEL)), _const_spec((D_FF, D_MODEL)),
                  _const_spec((1, D_MODEL))],
        out_specs=row(D_MODEL),
        out_shape=jax.ShapeDtypeStruct((batch, seq, D_MODEL), F32),
        scratch_shapes=[pool_buf, pool_buf, order_buf, order_buf, pltpu.VMEM((tm, D_MODEL), F32)],
        compiler_params=_params(2),
        name="mix_ffn",
    )(x, u, u, u, *attn, w_pool_bd, pool_scale, w_out, g_mix_post, g_pre, wg, wu, wd, g_post)


def _ffn_weights(w_gate, w_up, w_down):
    return w_gate.T.astype(BF16), w_up.T.astype(BF16), w_down.astype(BF16)


def _qkv_by_group(w_qkv):
    d_model = w_qkv.shape[0]
    n_groups = len(ATTN_GROUPS)
    return w_qkv.reshape(d_model, 3, n_groups, GROUP_DIM).transpose(0, 2, 1, 3).reshape(d_model, D_QKV)


def _block_diag(w_lin):
    g, c, e = w_lin.shape
    eye = jnp.eye(g, dtype=w_lin.dtype)
    return (eye[:, None, :, None] * w_lin[:, :, None, :]).reshape(g * c, g * e)


def kernel(x, g_ffn1_pre, w1_gate, w1_up, w1_down, g_ffn1_post, g_mix_pre, w_in, w_pool_lin, pool_scale,
           w_out, g_mix_post, g_ffn2_pre, w2_gate, w2_up, w2_down, g_ffn2_post):
    batch, seq, _ = x.shape
    depth = g_ffn1_pre.shape[0]
    tm = 512
    assert seq % tm == 0 and tm % (ATTN_GROUPS[-1][1] * 16) == 0
    h = x
    for l in range(depth):
        w_in_bf = w_in[l].astype(BF16)
        x1, u, *qkv = _ffn_proj(h, g_ffn1_pre[l][None], *_ffn_weights(w1_gate[l], w1_up[l], w1_down[l]),
                                g_ffn1_post[l][None], g_mix_pre[l][None], w_in_bf[:, :POOL_DIM],
                                _qkv_by_group(w_in_bf[:, POOL_DIM:]), tm=tm)
        attn = [_attention_group(qkv[gi], gi) for gi in range(len(ATTN_GROUPS))]
        h = _mix_ffn(x1, u, attn, _block_diag(w_pool_lin[l]).astype(BF16), pool_scale[l][None],
                     w_out[l].astype(BF16), g_mix_post[l][None], g_ffn2_pre[l][None],
                     *_ffn_weights(w2_gate[l], w2_up[l], w2_down[l]), g_ffn2_post[l][None], tm=tm)
    return h
```

```python
import functools

import numpy as np
import jax
import jax.numpy as jnp
from jax import lax
from jax.experimental import pallas as pl
from jax.experimental.pallas import tpu as pltpu

D_MODEL = 1024
HEAD_DIM = 64
POOL_WINDOWS = (2, 4, 8, 16)
POOL_GROUP_DIM = 64
POOL_DIM = len(POOL_WINDOWS) * POOL_GROUP_DIM
ATTN_GROUPS = ((128, 1), (512, 4), (2048, 16))
HEADS_PER_GROUP = 4
N_ATTN_HEADS = HEADS_PER_GROUP * len(ATTN_GROUPS)
GROUP_DIM = HEADS_PER_GROUP * HEAD_DIM
ATTN_DIM = N_ATTN_HEADS * HEAD_DIM
D_QKV = 3 * ATTN_DIM
QKV_GROUP_DIM = 3 * GROUP_DIM
D_FF = 2752
RMS_EPS = 1e-6

LANES = 128
V7X_MXU_DIM = 256
V7X_VMEM_LIMIT_BYTES = 56 * 1024 * 1024
N_SIDE = 64
Q_SUB = 128
K_WIN = Q_SUB + 2 * N_SIDE
ATTN_ROWS_PER_STEP = 4096
SPLIT_STRIDE = 4
MIX_ROWS = 128
PROJ_ROWS = 256
POOL_HALO = 8
MASKED = float("-inf")
LOG2_E = 1.4426950408889634
Q_SCALE = HEAD_DIM ** -0.5 * LOG2_E
NT_DIMS = (((1,), (1,)), ((), ()))

F32 = jnp.float32
BF16 = jnp.bfloat16


def _rmsnorm(x, g):
    ms = jnp.mean(x * x, axis=-1, keepdims=True)
    return x * lax.rsqrt(ms + RMS_EPS) * g


def _const_spec(shape):
    return pl.BlockSpec(shape, lambda *_: (0,) * len(shape), pipeline_mode=pl.Buffered(1))


def _params(n_axes):
    return pltpu.CompilerParams(dimension_semantics=("arbitrary",) * n_axes,
                                vmem_limit_bytes=V7X_VMEM_LIMIT_BYTES)


def _ffn_tile(x, gpre_ref, wg_ref, wu_ref, wd_ref, gpost_ref, acc_ref):
    xn = _rmsnorm(x, gpre_ref[...]).astype(BF16)
    for c0 in range(0, D_FF, V7X_MXU_DIM):
        c1 = min(c0 + V7X_MXU_DIM, D_FF)
        gate = lax.dot_general(xn, wg_ref[c0:c1, :], NT_DIMS, preferred_element_type=F32)
        up = lax.dot_general(xn, wu_ref[c0:c1, :], NT_DIMS, preferred_element_type=F32)
        act = (gate * jax.nn.sigmoid(gate) * up).astype(BF16)
        part = jnp.dot(act, wd_ref[c0:c1, :], preferred_element_type=F32)
        if c0 == 0:
            acc_ref[...] = part
        else:
            acc_ref[...] += part
    return x + 0.5 * _rmsnorm(acc_ref[...], gpost_ref[...])


def _ffn_proj_kernel(x_ref, gpre_ref, wg_ref, wu_ref, wd_ref, gpost_ref, gmix_ref, wpin_ref, wqkv_ref,
                     x1_ref, u_ref, q0_ref, q1_ref, q2_ref, acc_ref, zbuf, *, tm):
    x1 = _ffn_tile(x_ref[...], gpre_ref, wg_ref, wu_ref, wd_ref, gpost_ref, acc_ref)
    x1_ref[...] = x1
    n_slabs = QKV_GROUP_DIM // LANES
    for blk in range(tm // PROJ_ROWS):
        rows = slice(blk * PROJ_ROWS, (blk + 1) * PROJ_ROWS)
        h = _rmsnorm(x1[rows], gmix_ref[...]).astype(BF16)
        u_ref[rows, :] = jnp.dot(h, wpin_ref[...], preferred_element_type=F32)
        for gi, (out, (_, dilation)) in enumerate(zip((q0_ref, q1_ref, q2_ref), ATTN_GROUPS)):
            z = jnp.dot(h, wqkv_ref[:, gi * QKV_GROUP_DIM:(gi + 1) * QKV_GROUP_DIM], preferred_element_type=F32)
            z = jnp.concatenate([z[:, :GROUP_DIM] * Q_SCALE, z[:, GROUP_DIM:]], axis=-1)
            if dilation == 1:
                out[0, rows, :] = z.astype(BF16)
                continue
            for s in range(n_slabs):
                zbuf[blk, 0, s] = z[:, s * LANES:(s + 1) * LANES]
            part = PROJ_ROWS // SPLIT_STRIDE
            if dilation == SPLIT_STRIDE:
                for r in range(dilation):
                    for s in range(n_slabs):
                        out[r, blk * part:(blk + 1) * part, s * LANES:(s + 1) * LANES] = (
                            zbuf[blk, 0, s, pl.ds(r, part, stride=SPLIT_STRIDE), :].astype(BF16))
                continue
            assert dilation == SPLIT_STRIDE * SPLIT_STRIDE
            for r_lo in range(SPLIT_STRIDE):
                for s in range(n_slabs):
                    zbuf[blk, 1, s, r_lo * part:(r_lo + 1) * part, :] = (
                        zbuf[blk, 0, s, pl.ds(r_lo, part, stride=SPLIT_STRIDE), :])
            per_res = PROJ_ROWS // dilation
            for r_lo in range(SPLIT_STRIDE):
                for r_hi in range(SPLIT_STRIDE):
                    for s in range(n_slabs):
                        out[r_lo + SPLIT_STRIDE * r_hi, blk * per_res:(blk + 1) * per_res, s * LANES:(s + 1) * LANES] = (
                            zbuf[blk, 1, s, pl.ds(r_lo * part + r_hi, per_res, stride=SPLIT_STRIDE), :].astype(BF16))


def _ffn_proj(x, g_pre, wg, wu, wd, g_post, g_mix, w_pool_in, w_qkv, *, tm):
    batch, seq, _ = x.shape
    row = lambda width: pl.BlockSpec((None, tm, width), lambda b, i: (b, i, 0))
    qkv_specs = [pl.BlockSpec((None, d, tm // d, QKV_GROUP_DIM), lambda b, i: (b, 0, i, 0)) for _, d in ATTN_GROUPS]
    qkv_shapes = [jax.ShapeDtypeStruct((batch, d, seq // d, QKV_GROUP_DIM), BF16) for _, d in ATTN_GROUPS]
    return pl.pallas_call(
        functools.partial(_ffn_proj_kernel, tm=tm),
        grid=(batch, seq // tm),
        in_specs=[row(D_MODEL), _const_spec((1, D_MODEL)), _const_spec((D_FF, D_MODEL)),
                  _const_spec((D_FF, D_MODEL)), _const_spec((D_FF, D_MODEL)), _const_spec((1, D_MODEL)),
                  _const_spec((1, D_MODEL)), _const_spec((D_MODEL, POOL_DIM)), _const_spec((D_MODEL, D_QKV))],
        out_specs=[row(D_MODEL), row(POOL_DIM)] + qkv_specs,
        out_shape=[jax.ShapeDtypeStruct((batch, seq, D_MODEL), F32),
                   jax.ShapeDtypeStruct((batch, seq, POOL_DIM), F32)] + qkv_shapes,
        scratch_shapes=[pltpu.VMEM((tm, D_MODEL), F32),
                        pltpu.VMEM((tm // PROJ_ROWS, 2, QKV_GROUP_DIM // LANES, PROJ_ROWS, LANES), F32)],
        compiler_params=_params(2),
        name="ffn_proj",
    )(x, g_pre, wg, wu, wd, g_post, g_mix, w_pool_in, w_qkv)


def _attn_kernel(cur_ref, prev_ref, next_ref, out_ref,
                 kwin, vwin, qs_ref, bias_ref, *, dilation, slopes, sub_len, tq, n_res):
    first = (pl.program_id(0) == 0) & (pl.program_id(1) == 0) & (pl.program_id(2) == 0)

    @pl.when(first)
    def _():
        qi = lax.broadcasted_iota(jnp.int32, (Q_SUB, K_WIN), 0)
        kj = lax.broadcasted_iota(jnp.int32, (Q_SUB, K_WIN), 1)
        rel = jnp.abs(kj - N_SIDE - qi)
        dist = (rel * dilation).astype(F32)
        in_band = rel <= N_SIDE
        for variant in range(4):
            ok = in_band
            if variant & 1:
                ok = ok & (kj >= N_SIDE)
            if variant & 2:
                ok = ok & (kj < N_SIDE + Q_SUB)
            for h in range(HEADS_PER_GROUP):
                bias_ref[variant, h * Q_SUB:(h + 1) * Q_SUB, :] = jnp.where(ok, -slopes[h] * dist * LOG2_E, MASKED)

    k_cols, v_cols = slice(GROUP_DIM, 2 * GROUP_DIM), slice(2 * GROUP_DIM, 3 * GROUP_DIM)
    for win, cols in ((kwin, k_cols), (vwin, v_cols)):
        win[:, 0:N_SIDE] = prev_ref[:, :, cols]
        win[:, N_SIDE:N_SIDE + tq] = cur_ref[:, :, cols]
        win[:, N_SIDE + tq:] = next_ref[:, :, cols]

    n_sub = tq // Q_SUB
    sub0 = pl.program_id(2) * n_sub
    last_sub = sub_len // Q_SUB - 1
    lane = lax.broadcasted_iota(jnp.int32, (1, GROUP_DIM), 1)
    head_of_lane = lane >> 6
    even_head = lax.broadcasted_iota(jnp.int32, (1, LANES), 1) < HEAD_DIM

    for res in range(n_res):
        for j in range(n_sub):
            rows = slice(j * Q_SUB, (j + 1) * Q_SUB)
            q = cur_ref[res, rows, 0:GROUP_DIM]
            for h in range(HEADS_PER_GROUP):
                qs_ref[h * Q_SUB:(h + 1) * Q_SUB, :] = q * (head_of_lane == h).astype(BF16)
            kw = kwin[res, j * Q_SUB:j * Q_SUB + K_WIN, :]
            vw = vwin[res, j * Q_SUB:j * Q_SUB + K_WIN, :]
            g = sub0 + j
            variant = (g == 0).astype(jnp.int32) + 2 * (g == last_sub).astype(jnp.int32)
            s = lax.dot_general(qs_ref[...], kw, (((1,), (1,)), ((), ())), preferred_element_type=F32)
            s = s + bias_ref[variant]
            m = jnp.max(s, axis=-1, keepdims=True)
            p = jnp.exp2(s - m)
            den = jnp.sum(p, axis=-1, keepdims=True)
            pb = p.astype(BF16)
            p_wide = jnp.concatenate([pb[h * Q_SUB:(h + 1) * Q_SUB] for h in range(HEADS_PER_GROUP)], axis=-1)
            v_tall = jnp.concatenate([vw * (head_of_lane == h).astype(BF16) for h in range(HEADS_PER_GROUP)], axis=0)
            o = jnp.dot(p_wide, v_tall, preferred_element_type=F32)
            for half in range(2):
                cols = slice(half * LANES, (half + 1) * LANES)
                ra = slice(2 * half * Q_SUB, (2 * half + 1) * Q_SUB)
                rb = slice((2 * half + 1) * Q_SUB, (2 * half + 2) * Q_SUB)
                den_half = jnp.where(even_head, den[ra], den[rb])
                m_half = jnp.where(even_head, m[ra], m[rb])
                out_ref[res, rows, cols] = o[:, cols] / den_half
                out_ref[res, rows, GROUP_DIM + half * LANES:GROUP_DIM + (half + 1) * LANES] = (
                    m_half + jnp.log2(den_half))


def _attention_group(qkv, group):
    batch, dilation, sub_len, _ = qkv.shape
    tq = min(ATTN_ROWS_PER_STEP, sub_len)
    n_res = min(ATTN_ROWS_PER_STEP // tq, dilation)
    assert sub_len % tq == 0 and tq % Q_SUB == 0 and dilation % n_res == 0
    halo_per_tq = tq // N_SIDE
    n_halo = sub_len // N_SIDE
    slopes = tuple(float(np.float32(2.0 ** (-8.0 * (group * HEADS_PER_GROUP + h + 1) / N_ATTN_HEADS)))
                   for h in range(HEADS_PER_GROUP))

    cur = pl.BlockSpec((None, n_res, tq, QKV_GROUP_DIM), lambda b, r, i: (b, r, i, 0))
    prev = pl.BlockSpec((None, n_res, N_SIDE, QKV_GROUP_DIM),
                        lambda b, r, i: (b, r, jnp.maximum(i * halo_per_tq - 1, 0), 0))
    nxt = pl.BlockSpec((None, n_res, N_SIDE, QKV_GROUP_DIM),
                       lambda b, r, i: (b, r, jnp.minimum((i + 1) * halo_per_tq, n_halo - 1), 0))
    return pl.pallas_call(
        functools.partial(_attn_kernel, dilation=dilation, slopes=slopes, sub_len=sub_len, tq=tq, n_res=n_res),
        grid=(batch, dilation // n_res, sub_len // tq),
        in_specs=[cur, prev, nxt],
        out_specs=pl.BlockSpec((None, n_res, tq, 2 * GROUP_DIM), lambda b, r, i: (b, r, i, 0)),
        out_shape=jax.ShapeDtypeStruct((batch, dilation, sub_len, 2 * GROUP_DIM), F32),
        scratch_shapes=[pltpu.VMEM((n_res, tq + 2 * N_SIDE, GROUP_DIM), BF16),
                        pltpu.VMEM((n_res, tq + 2 * N_SIDE, GROUP_DIM), BF16),
                        pltpu.VMEM((HEADS_PER_GROUP * Q_SUB, GROUP_DIM), BF16),
                        pltpu.VMEM((4, HEADS_PER_GROUP * Q_SUB, K_WIN), F32)],
        compiler_params=_params(3),
        name=f"attn_d{dilation}",
    )(qkv, qkv, qkv)


def _pool_mean_minus_token(u_ref, up_ref, un_ref, ubuf, abuf, *, tm, seq):
    tiles_per_seq = seq // tm
    ti = pl.program_id(1)
    n = tm + 2 * POOL_HALO
    zero_halo = jnp.zeros((POOL_HALO, POOL_DIM), F32)
    ubuf[0:POOL_HALO] = jnp.where(ti == 0, zero_halo, up_ref[...])
    ubuf[POOL_HALO:POOL_HALO + tm] = u_ref[...]
    ubuf[POOL_HALO + tm:n] = jnp.where(ti == tiles_per_seq - 1, zero_halo, un_ref[...])
    ubuf[n:] = zero_halo
    abuf[n:] = zero_halo
    lo, hi = slice(0, LANES), slice(LANES, 2 * LANES)
    a2 = ubuf[0:n, lo] + ubuf[1:n + 1, lo]
    abuf[0:n, lo] = a2
    s2 = abuf[7:7 + tm, lo]
    ubuf[0:n, lo] = a2 + abuf[2:n + 2, lo]
    s4 = ubuf[6:6 + tm, lo]
    a2 = ubuf[0:n, hi] + ubuf[1:n + 1, hi]
    abuf[0:n, hi] = a2
    a4 = a2 + abuf[2:n + 2, hi]
    ubuf[0:n, hi] = a4
    a8 = a4 + ubuf[4:n + 4, hi]
    abuf[0:n, hi] = a8
    s8 = abuf[4:4 + tm, hi]
    s16 = a8[0:tm] + abuf[8:8 + tm, hi]
    first_group = lax.broadcasted_iota(jnp.int32, (1, LANES), 1) < POOL_GROUP_DIM
    edge_row = lax.broadcasted_iota(jnp.int32, (POOL_HALO, 1), 0)

    def mean(sum_a, sum_b, half_a, half_b):
        half = jnp.where(first_group, half_a, half_b)
        win_sum = jnp.where(first_group, sum_a, sum_b)

        def edge(r0):
            t = ti * tm + r0 + edge_row
            count = jnp.minimum(t + half, seq) - jnp.maximum(t - half, 0)
            return win_sum[r0:r0 + POOL_HALO] / count.astype(F32)

        inv_window = jnp.where(first_group, 0.5 / half_a, 0.5 / half_b)
        return jnp.concatenate([edge(0), win_sum[POOL_HALO:tm - POOL_HALO] * inv_window, edge(tm - POOL_HALO)],
                               axis=0)

    return jnp.concatenate([mean(s2, s4, 1, 2), mean(s8, s16, 4, 8)], axis=-1) - u_ref[...]


def _token_order(blk_ref, buf, *, tm):
    dilation = blk_ref.shape[0]
    if dilation == 1:
        return blk_ref[0]
    n_slabs = blk_ref.shape[-1] // LANES
    part = tm // SPLIT_STRIDE
    if dilation == SPLIT_STRIDE:
        for r in range(dilation):
            for s in range(n_slabs):
                buf[0, s, pl.ds(r, part, stride=SPLIT_STRIDE), :] = blk_ref[r, :, s * LANES:(s + 1) * LANES]
        return jnp.concatenate([buf[0, s] for s in range(n_slabs)], axis=-1)
    assert dilation == SPLIT_STRIDE * SPLIT_STRIDE
    for r_lo in range(SPLIT_STRIDE):
        for r_hi in range(SPLIT_STRIDE):
            for s in range(n_slabs):
                buf[1, s, pl.ds(r_lo * part + r_hi, tm // dilation, stride=SPLIT_STRIDE), :] = (
                    blk_ref[r_lo + SPLIT_STRIDE * r_hi, :, s * LANES:(s + 1) * LANES])
    for r_lo in range(SPLIT_STRIDE):
        for s in range(n_slabs):
            buf[0, s, pl.ds(r_lo, part, stride=SPLIT_STRIDE), :] = buf[1, s, r_lo * part:(r_lo + 1) * part, :]
    return jnp.concatenate([buf[0, s] for s in range(n_slabs)], axis=-1)


def _mix_ffn_kernel(x_ref, u_ref, up_ref, un_ref, a1_ref, a2_ref, a3_ref,
                    wpool_ref, pscale_ref, wout_ref, gmix_ref, gpre_ref, wg_ref, wu_ref, wd_ref, gpost_ref,
                    out_ref, ubuf, abuf, buf2, buf3, acc_ref, *, tm, seq):
    y = _pool_mean_minus_token(u_ref, up_ref, un_ref, ubuf, abuf, tm=tm, seq=seq)
    a_pool = jnp.dot(y.astype(BF16), wpool_ref[...], preferred_element_type=F32) * pscale_ref[...]
    groups = [a1_ref[0], _token_order(a2_ref, buf2, tm=tm), _token_order(a3_ref, buf3, tm=tm)]
    x2_parts = []
    for r0 in range(0, tm, MIX_ROWS):
        rows = slice(r0, r0 + MIX_ROWS)
        outs = [g[rows, :GROUP_DIM] for g in groups]
        l1, l2, l3 = (g[rows, GROUP_DIM:] for g in groups)
        m = jnp.maximum(jnp.maximum(l1, l2), l3)
        e1, e2, e3 = jnp.exp2(l1 - m), jnp.exp2(l2 - m), jnp.exp2(l3 - m)
        inv_z = 1.0 / (e1 + e2 + e3)
        cat = jnp.concatenate([a_pool[rows]] + [o * (e * inv_z) for o, e in zip(outs, (e1, e2, e3))],
                              axis=-1).astype(BF16)
        mix = jnp.dot(cat, wout_ref[...], preferred_element_type=F32)
        x2_parts.append(x_ref[rows, :] + _rmsnorm(mix, gmix_ref[...]))
    x2 = jnp.concatenate(x2_parts, axis=0)
    out_ref[...] = _ffn_tile(x2, gpre_ref, wg_ref, wu_ref, wd_ref, gpost_ref, acc_ref)


def _mix_ffn(x, u, attn, w_pool_bd, pool_scale, w_out, g_mix_post, g_pre, wg, wu, wd, g_post, *, tm):
    batch, seq, _ = x.shape
    n_halo = seq // POOL_HALO
    halo_per_tm = tm // POOL_HALO
    row = lambda width: pl.BlockSpec((None, tm, width), lambda b, i: (b, i, 0))
    prev = pl.BlockSpec((None, POOL_HALO, POOL_DIM), lambda b, i: (b, jnp.maximum(i * halo_per_tm - 1, 0), 0))
    nxt = pl.BlockSpec((None, POOL_HALO, POOL_DIM),
                       lambda b, i: (b, jnp.minimum((i + 1) * halo_per_tm, n_halo - 1), 0))
    by_residue = [pl.BlockSpec((None, d, tm // d, 2 * GROUP_DIM), lambda b, i: (b, 0, i, 0)) for _, d in ATTN_GROUPS]
    pool_buf = pltpu.VMEM((tm + 3 * POOL_HALO, POOL_DIM), F32)
    order_buf = pltpu.VMEM((2, 2 * GROUP_DIM // LANES, tm, LANES), F32)
    return pl.pallas_call(
        functools.partial(_mix_ffn_kernel, tm=tm, seq=seq),
        grid=(batch, seq // tm),
        in_specs=[row(D_MODEL), row(POOL_DIM), prev, nxt] + by_residue +
                 [_const_spec((POOL_DIM, POOL_DIM)), _const_spec((1, POOL_DIM)),
                  _const_spec((D_MODEL, D_MODEL)), _const_spec((1, D_MODEL)), _const_spec((1, D_MODEL)),
                  _const_spec((D_FF, D_MODEL)), _const_spec((D_FF, D_MODEL)), _const_spec((D_FF, D_MODEL)),
                  _const_spec((1, D_MODEL))],
        out_specs=row(D_MODEL),
        out_shape=jax.ShapeDtypeStruct((batch, seq, D_MODEL), F32),
        scratch_shapes=[pool_buf, pool_buf, order_buf, order_buf, pltpu.VMEM((tm, D_MODEL), F32)],
        compiler_params=_params(2),
        name="mix_ffn",
    )(x, u, u, u, *attn, w_pool_bd, pool_scale, w_out, g_mix_post, g_pre, wg, wu, wd, g_post)


def _ffn_weights(w_gate, w_up, w_down):
    return w_gate.T.astype(BF16), w_up.T.astype(BF16), w_down.astype(BF16)


def _qkv_by_group(w_qkv):
    d_model = w_qkv.shape[0]
    n_groups = len(ATTN_GROUPS)
    return w_qkv.reshape(d_model, 3, n_groups, GROUP_DIM).transpose(0, 2, 1, 3).reshape(d_model, D_QKV)


def _block_diag(w_lin):
    g, c, e = w_lin.shape
    eye = jnp.eye(g, dtype=w_lin.dtype)
    return (eye[:, None, :, None] * w_lin[:, :, None, :]).reshape(g * c, g * e)


def kernel(x, g_ffn1_pre, w1_gate, w1_up, w1_down, g_ffn1_post, g_mix_pre, w_in, w_pool_lin, pool_scale,
           w_out, g_mix_post, g_ffn2_pre, w2_gate, w2_up, w2_down, g_ffn2_post):
    batch, seq, _ = x.shape
    depth = g_ffn1_pre.shape[0]
    tm = 512
    assert seq % tm == 0 and tm % (ATTN_GROUPS[-1][1] * 16) == 0
    h = x
    for l in range(depth):
        w_in_bf = w_in[l].astype(BF16)
        x1, u, *qkv = _ffn_proj(h, g_ffn1_pre[l][None], *_ffn_weights(w1_gate[l], w1_up[l], w1_down[l]),
                                g_ffn1_post[l][None], g_mix_pre[l][None], w_in_bf[:, :POOL_DIM],
                                _qkv_by_group(w_in_bf[:, POOL_DIM:]), tm=tm)
        attn = [_attention_group(qkv[gi], gi) for gi in range(len(ATTN_GROUPS))]
        h = _mix_ffn(x1, u, attn, _block_diag(w_pool_lin[l]).astype(BF16), pool_scale[l][None],
                     w_out[l].astype(BF16), g_mix_post[l][None], g_ffn2_pre[l][None],
                     *_ffn_weights(w2_gate[l], w2_up[l], w2_down[l]), g_ffn2_post[l][None], tm=tm)
    return h
```

```python
import functools

import numpy as np
import jax
import jax.numpy as jnp
from jax import lax
from jax.experimental import pallas as pl
from jax.experimental.pallas import tpu as pltpu

D_MODEL = 1024
HEAD_DIM = 64
POOL_WINDOWS = (2, 4, 8, 16)
POOL_GROUP_DIM = 64
POOL_DIM = len(POOL_WINDOWS) * POOL_GROUP_DIM
ATTN_GROUPS = ((128, 1), (512, 4), (2048, 16))
HEADS_PER_GROUP = 4
N_ATTN_HEADS = HEADS_PER_GROUP * len(ATTN_GROUPS)
GROUP_DIM = HEADS_PER_GROUP * HEAD_DIM
ATTN_DIM = N_ATTN_HEADS * HEAD_DIM
D_QKV = 3 * ATTN_DIM
QKV_GROUP_DIM = 3 * GROUP_DIM
D_FF = 2752
RMS_EPS = 1e-6

LANES = 128
V7X_MXU_DIM = 256
V7X_VMEM_LIMIT_BYTES = 56 * 1024 * 1024
N_SIDE = 64
Q_SUB = 128
K_WIN = Q_SUB + 2 * N_SIDE
ATTN_ROWS_PER_STEP = 4096
SPLIT_STRIDE = 4
POOL_HALO = 8
MASKED = float("-inf")
LOG2_E = 1.4426950408889634
Q_SCALE = HEAD_DIM ** -0.5 * LOG2_E
NT_DIMS = (((1,), (1,)), ((), ()))

F32 = jnp.float32
BF16 = jnp.bfloat16


def _rmsnorm(x, g):
    ms = jnp.mean(x * x, axis=-1, keepdims=True)
    return x * lax.rsqrt(ms + RMS_EPS) * g


def _const_spec(shape):
    return pl.BlockSpec(shape, lambda *_: (0,) * len(shape), pipeline_mode=pl.Buffered(1))


def _params(n_axes):
    return pltpu.CompilerParams(dimension_semantics=("arbitrary",) * n_axes,
                                vmem_limit_bytes=V7X_VMEM_LIMIT_BYTES)


def _ffn_tile(x, gpre_ref, wg_ref, wu_ref, wd_ref, gpost_ref, acc_ref):
    xn = _rmsnorm(x, gpre_ref[...]).astype(BF16)
    for c0 in range(0, D_FF, V7X_MXU_DIM):
        c1 = min(c0 + V7X_MXU_DIM, D_FF)
        gate = lax.dot_general(xn, wg_ref[c0:c1, :], NT_DIMS, preferred_element_type=F32)
        up = lax.dot_general(xn, wu_ref[c0:c1, :], NT_DIMS, preferred_element_type=F32)
        act = (gate * jax.nn.sigmoid(gate) * up).astype(BF16)
        part = jnp.dot(act, wd_ref[c0:c1, :], preferred_element_type=F32)
        if c0 == 0:
            acc_ref[...] = part
        else:
            acc_ref[...] += part
    return x + _rmsnorm(acc_ref[...], 0.5 * gpost_ref[...])


def _ffn_proj_kernel(x_ref, gpre_ref, wg_ref, wu_ref, wd_ref, gpost_ref, gmix_ref, wpin_ref, wqkv_ref,
                     x1_ref, u_ref, q0_ref, q1_ref, q2_ref, acc_ref, zbuf, *, tm):
    x1 = _ffn_tile(x_ref[...], gpre_ref, wg_ref, wu_ref, wd_ref, gpost_ref, acc_ref)
    x1_ref[...] = x1
    h = _rmsnorm(x1, gmix_ref[...]).astype(BF16)
    u_ref[...] = jnp.dot(h, wpin_ref[...], preferred_element_type=F32)
    n_slabs = QKV_GROUP_DIM // LANES
    for gi, (out, (_, dilation)) in enumerate(zip((q0_ref, q1_ref, q2_ref), ATTN_GROUPS)):
        z = jnp.dot(h, wqkv_ref[:, gi * QKV_GROUP_DIM:(gi + 1) * QKV_GROUP_DIM], preferred_element_type=F32)
        z = jnp.concatenate([z[:, :GROUP_DIM] * Q_SCALE, z[:, GROUP_DIM:]], axis=-1)
        if dilation == 1:
            out[0] = z.astype(BF16)
            continue
        for s in range(n_slabs):
            zbuf[0, s] = z[:, s * LANES:(s + 1) * LANES]
        part = tm // SPLIT_STRIDE
        if dilation == SPLIT_STRIDE:
            for r in range(dilation):
                for s in range(n_slabs):
                    out[r, :, s * LANES:(s + 1) * LANES] = (
                        zbuf[0, s, pl.ds(r, part, stride=SPLIT_STRIDE), :].astype(BF16))
            continue
        assert dilation == SPLIT_STRIDE * SPLIT_STRIDE
        for r_lo in range(SPLIT_STRIDE):
            for s in range(n_slabs):
                zbuf[1, s, r_lo * part:(r_lo + 1) * part, :] = zbuf[0, s, pl.ds(r_lo, part, stride=SPLIT_STRIDE), :]
        for r_lo in range(SPLIT_STRIDE):
            for r_hi in range(SPLIT_STRIDE):
                for s in range(n_slabs):
                    out[r_lo + SPLIT_STRIDE * r_hi, :, s * LANES:(s + 1) * LANES] = (
                        zbuf[1, s, pl.ds(r_lo * part + r_hi, tm // dilation, stride=SPLIT_STRIDE), :].astype(BF16))


def _ffn_proj(x, g_pre, wg, wu, wd, g_post, g_mix, w_pool_in, w_qkv, *, tm):
    batch, seq, _ = x.shape
    row = lambda width: pl.BlockSpec((None, tm, width), lambda b, i: (b, i, 0))
    qkv_specs = [pl.BlockSpec((None, d, tm // d, QKV_GROUP_DIM), lambda b, i: (b, 0, i, 0)) for _, d in ATTN_GROUPS]
    qkv_shapes = [jax.ShapeDtypeStruct((batch, d, seq // d, QKV_GROUP_DIM), BF16) for _, d in ATTN_GROUPS]
    return pl.pallas_call(
        functools.partial(_ffn_proj_kernel, tm=tm),
        grid=(batch, seq // tm),
        in_specs=[row(D_MODEL), _const_spec((1, D_MODEL)), _const_spec((D_FF, D_MODEL)),
                  _const_spec((D_FF, D_MODEL)), _const_spec((D_FF, D_MODEL)), _const_spec((1, D_MODEL)),
                  _const_spec((1, D_MODEL)), _const_spec((D_MODEL, POOL_DIM)), _const_spec((D_MODEL, D_QKV))],
        out_specs=[row(D_MODEL), row(POOL_DIM)] + qkv_specs,
        out_shape=[jax.ShapeDtypeStruct((batch, seq, D_MODEL), F32),
                   jax.ShapeDtypeStruct((batch, seq, POOL_DIM), F32)] + qkv_shapes,
        scratch_shapes=[pltpu.VMEM((tm, D_MODEL), F32),
                        pltpu.VMEM((len(ATTN_GROUPS) - 1, QKV_GROUP_DIM // LANES, tm, LANES), F32)],
        compiler_params=_params(2),
        name="ffn_proj",
    )(x, g_pre, wg, wu, wd, g_post, g_mix, w_pool_in, w_qkv)


def _attn_kernel(cur_ref, prev_ref, next_ref, out_ref,
                 kwin, vwin, qs_ref, bias_ref, *, dilation, slopes, sub_len, tq, n_res):
    first = (pl.program_id(0) == 0) & (pl.program_id(1) == 0) & (pl.program_id(2) == 0)

    @pl.when(first)
    def _():
        qi = lax.broadcasted_iota(jnp.int32, (Q_SUB, K_WIN), 0)
        kj = lax.broadcasted_iota(jnp.int32, (Q_SUB, K_WIN), 1)
        rel = jnp.abs(kj - N_SIDE - qi)
        dist = (rel * dilation).astype(F32)
        in_band = rel <= N_SIDE
        for variant in range(4):
            ok = in_band
            if variant & 1:
                ok = ok & (kj >= N_SIDE)
            if variant & 2:
                ok = ok & (kj < N_SIDE + Q_SUB)
            for h in range(HEADS_PER_GROUP):
                bias_ref[variant, h * Q_SUB:(h + 1) * Q_SUB, :] = jnp.where(ok, -slopes[h] * dist * LOG2_E, MASKED)

    k_cols, v_cols = slice(GROUP_DIM, 2 * GROUP_DIM), slice(2 * GROUP_DIM, 3 * GROUP_DIM)
    for win, cols in ((kwin, k_cols), (vwin, v_cols)):
        win[:, 0:N_SIDE] = prev_ref[:, :, cols]
        win[:, N_SIDE:N_SIDE + tq] = cur_ref[:, :, cols]
        win[:, N_SIDE + tq:] = next_ref[:, :, cols]

    n_sub = tq // Q_SUB
    sub0 = pl.program_id(2) * n_sub
    last_sub = sub_len // Q_SUB - 1
    lane = lax.broadcasted_iota(jnp.int32, (1, GROUP_DIM), 1)
    head_of_lane = lane >> 6
    even_head = lax.broadcasted_iota(jnp.int32, (1, LANES), 1) < HEAD_DIM

    for res in range(n_res):
        for j in range(n_sub):
            rows = slice(j * Q_SUB, (j + 1) * Q_SUB)
            q = cur_ref[res, rows, 0:GROUP_DIM]
            for h in range(HEADS_PER_GROUP):
                qs_ref[h * Q_SUB:(h + 1) * Q_SUB, :] = q * (head_of_lane == h).astype(BF16)
            kw = kwin[res, j * Q_SUB:j * Q_SUB + K_WIN, :]
            vw = vwin[res, j * Q_SUB:j * Q_SUB + K_WIN, :]
            g = sub0 + j
            variant = (g == 0).astype(jnp.int32) + 2 * (g == last_sub).astype(jnp.int32)
            s = lax.dot_general(qs_ref[...], kw, (((1,), (1,)), ((), ())), preferred_element_type=F32)
            s = s + bias_ref[variant]
            m = jnp.max(s, axis=-1, keepdims=True)
            p = jnp.exp2(s - m)
            den = jnp.sum(p, axis=-1, keepdims=True)
            pb = p.astype(BF16)
            p_wide = jnp.concatenate([pb[h * Q_SUB:(h + 1) * Q_SUB] for h in range(HEADS_PER_GROUP)], axis=-1)
            v_tall = jnp.concatenate([vw * (head_of_lane == h).astype(BF16) for h in range(HEADS_PER_GROUP)], axis=0)
            o = jnp.dot(p_wide, v_tall, preferred_element_type=F32)
            for half in range(2):
                cols = slice(half * LANES, (half + 1) * LANES)
                ra = slice(2 * half * Q_SUB, (2 * half + 1) * Q_SUB)
                rb = slice((2 * half + 1) * Q_SUB, (2 * half + 2) * Q_SUB)
                den_half = jnp.where(even_head, den[ra], den[rb])
                m_half = jnp.where(even_head, m[ra], m[rb])
                out_ref[res, rows, cols] = o[:, cols] / den_half
                out_ref[res, rows, GROUP_DIM + half * LANES:GROUP_DIM + (half + 1) * LANES] = (
                    m_half + jnp.log2(den_half))


def _attention_group(qkv, group):
    batch, dilation, sub_len, _ = qkv.shape
    tq = min(ATTN_ROWS_PER_STEP, sub_len)
    n_res = min(ATTN_ROWS_PER_STEP // tq, dilation)
    window = ATTN_GROUPS[group][0]
    assert dilation == ATTN_GROUPS[group][1] and (window // 2) // dilation == N_SIDE
    assert sub_len % tq == 0 and tq % Q_SUB == 0 and dilation % n_res == 0
    halo_per_tq = tq // N_SIDE
    n_halo = sub_len // N_SIDE
    slopes = tuple(float(np.float32(2.0 ** (-8.0 * (group * HEADS_PER_GROUP + h + 1) / N_ATTN_HEADS)))
                   for h in range(HEADS_PER_GROUP))

    cur = pl.BlockSpec((None, n_res, tq, QKV_GROUP_DIM), lambda b, r, i: (b, r, i, 0))
    prev = pl.BlockSpec((None, n_res, N_SIDE, QKV_GROUP_DIM),
                        lambda b, r, i: (b, r, jnp.maximum(i * halo_per_tq - 1, 0), 0))
    nxt = pl.BlockSpec((None, n_res, N_SIDE, QKV_GROUP_DIM),
                       lambda b, r, i: (b, r, jnp.minimum((i + 1) * halo_per_tq, n_halo - 1), 0))
    return pl.pallas_call(
        functools.partial(_attn_kernel, dilation=dilation, slopes=slopes, sub_len=sub_len, tq=tq, n_res=n_res),
        grid=(batch, dilation // n_res, sub_len // tq),
        in_specs=[cur, prev, nxt],
        out_specs=pl.BlockSpec((None, n_res, tq, 2 * GROUP_DIM), lambda b, r, i: (b, r, i, 0)),
        out_shape=jax.ShapeDtypeStruct((batch, dilation, sub_len, 2 * GROUP_DIM), F32),
        scratch_shapes=[pltpu.VMEM((n_res, tq + 2 * N_SIDE, GROUP_DIM), BF16),
                        pltpu.VMEM((n_res, tq + 2 * N_SIDE, GROUP_DIM), BF16),
                        pltpu.VMEM((HEADS_PER_GROUP * Q_SUB, GROUP_DIM), BF16),
                        pltpu.VMEM((4, HEADS_PER_GROUP * Q_SUB, K_WIN), F32)],
        compiler_params=_params(3),
        name=f"attn_d{dilation}",
    )(qkv, qkv, qkv)


def _pool_mean_minus_token(u_ref, up_ref, un_ref, ubuf, abuf, *, tm, seq):
    tiles_per_seq = seq // tm
    ti = pl.program_id(1)
    n = tm + 2 * POOL_HALO
    zero_halo = jnp.zeros((POOL_HALO, POOL_DIM), F32)
    ubuf[0:POOL_HALO] = jnp.where(ti == 0, zero_halo, up_ref[...])
    ubuf[POOL_HALO:POOL_HALO + tm] = u_ref[...]
    ubuf[POOL_HALO + tm:n] = jnp.where(ti == tiles_per_seq - 1, zero_halo, un_ref[...])
    ubuf[n:] = zero_halo
    abuf[n:] = zero_halo
    lo, hi = slice(0, LANES), slice(LANES, 2 * LANES)
    a2 = ubuf[0:n, lo] + ubuf[1:n + 1, lo]
    abuf[0:n, lo] = a2
    s2 = abuf[7:7 + tm, lo]
    ubuf[0:n, lo] = a2 + abuf[2:n + 2, lo]
    s4 = ubuf[6:6 + tm, lo]
    a2 = ubuf[0:n, hi] + ubuf[1:n + 1, hi]
    abuf[0:n, hi] = a2
    a4 = a2 + abuf[2:n + 2, hi]
    ubuf[0:n, hi] = a4
    a8 = a4 + ubuf[4:n + 4, hi]
    abuf[0:n, hi] = a8
    s8 = abuf[4:4 + tm, hi]
    s16 = a8[0:tm] + abuf[8:8 + tm, hi]
    first_group = lax.broadcasted_iota(jnp.int32, (1, LANES), 1) < POOL_GROUP_DIM
    edge_row = lax.broadcasted_iota(jnp.int32, (POOL_HALO, 1), 0)

    def mean(sum_a, sum_b, half_a, half_b):
        half = jnp.where(first_group, half_a, half_b)
        win_sum = jnp.where(first_group, sum_a, sum_b)

        def edge(r0):
            t = ti * tm + r0 + edge_row
            count = jnp.minimum(t + half, seq) - jnp.maximum(t - half, 0)
            return win_sum[r0:r0 + POOL_HALO] / count.astype(F32)

        inv_window = jnp.where(first_group, 0.5 / half_a, 0.5 / half_b)
        return jnp.concatenate([edge(0), win_sum[POOL_HALO:tm - POOL_HALO] * inv_window, edge(tm - POOL_HALO)],
                               axis=0)

    return jnp.concatenate([mean(s2, s4, 1, 2), mean(s8, s16, 4, 8)], axis=-1) - u_ref[...]


def _token_order(blk_ref, buf, *, tm):
    dilation = blk_ref.shape[0]
    if dilation == 1:
        return blk_ref[0]
    n_slabs = blk_ref.shape[-1] // LANES
    part = tm // SPLIT_STRIDE
    if dilation == SPLIT_STRIDE:
        for r in range(dilation):
            for s in range(n_slabs):
                buf[0, s, pl.ds(r, part, stride=SPLIT_STRIDE), :] = blk_ref[r, :, s * LANES:(s + 1) * LANES]
        return jnp.concatenate([buf[0, s] for s in range(n_slabs)], axis=-1)
    assert dilation == SPLIT_STRIDE * SPLIT_STRIDE
    for r_lo in range(SPLIT_STRIDE):
        for r_hi in range(SPLIT_STRIDE):
            for s in range(n_slabs):
                buf[1, s, pl.ds(r_lo * part + r_hi, tm // dilation, stride=SPLIT_STRIDE), :] = (
                    blk_ref[r_lo + SPLIT_STRIDE * r_hi, :, s * LANES:(s + 1) * LANES])
    for r_lo in range(SPLIT_STRIDE):
        for s in range(n_slabs):
            buf[0, s, pl.ds(r_lo, part, stride=SPLIT_STRIDE), :] = buf[1, s, r_lo * part:(r_lo + 1) * part, :]
    return jnp.concatenate([buf[0, s] for s in range(n_slabs)], axis=-1)


def _mix_ffn_kernel(x_ref, u_ref, up_ref, un_ref, a1_ref, a2_ref, a3_ref,
                    wpool_ref, pscale_ref, wout_ref, gmix_ref, gpre_ref, wg_ref, wu_ref, wd_ref, gpost_ref,
                    out_ref, ubuf, abuf, buf2, buf3, acc_ref, *, tm, seq):
    y = _pool_mean_minus_token(u_ref, up_ref, un_ref, ubuf, abuf, tm=tm, seq=seq)
    a_pool = jnp.dot(y.astype(BF16), wpool_ref[...], preferred_element_type=F32) * pscale_ref[...]
    groups = [a1_ref[0], _token_order(a2_ref, buf2, tm=tm), _token_order(a3_ref, buf3, tm=tm)]
    outs = [g[:, :GROUP_DIM] for g in groups]
    l1, l2, l3 = (g[:, GROUP_DIM:] for g in groups)
    m = jnp.maximum(jnp.maximum(l1, l2), l3)
    e1, e2, e3 = jnp.exp2(l1 - m), jnp.exp2(l2 - m), jnp.exp2(l3 - m)
    inv_z = 1.0 / (e1 + e2 + e3)
    cat = jnp.concatenate([a_pool] + [o * (e * inv_z) for o, e in zip(outs, (e1, e2, e3))], axis=-1).astype(BF16)
    mix = jnp.dot(cat, wout_ref[...], preferred_element_type=F32)
    x2 = x_ref[...] + _rmsnorm(mix, gmix_ref[...])
    out_ref[...] = _ffn_tile(x2, gpre_ref, wg_ref, wu_ref, wd_ref, gpost_ref, acc_ref)


def _mix_ffn(x, u, attn, w_pool_bd, pool_scale, w_out, g_mix_post, g_pre, wg, wu, wd, g_post, *, tm):
    batch, seq, _ = x.shape
    n_halo = seq // POOL_HALO
    halo_per_tm = tm // POOL_HALO
    row = lambda width: pl.BlockSpec((None, tm, width), lambda b, i: (b, i, 0))
    prev = pl.BlockSpec((None, POOL_HALO, POOL_DIM), lambda b, i: (b, jnp.maximum(i * halo_per_tm - 1, 0), 0))
    nxt = pl.BlockSpec((None, POOL_HALO, POOL_DIM),
                       lambda b, i: (b, jnp.minimum((i + 1) * halo_per_tm, n_halo - 1), 0))
    by_residue = [pl.BlockSpec((None, d, tm // d, 2 * GROUP_DIM), lambda b, i: (b, 0, i, 0)) for _, d in ATTN_GROUPS]
    pool_buf = pltpu.VMEM((tm + 3 * POOL_HALO, POOL_DIM), F32)
    order_buf = pltpu.VMEM((2, 2 * GROUP_DIM // LANES, tm, LANES), F32)
    return pl.pallas_call(
        functools.partial(_mix_ffn_kernel, tm=tm, seq=seq),
        grid=(batch, seq // tm),
        in_specs=[row(D_MODEL), row(POOL_DIM), prev, nxt] + by_residue +
                 [_const_spec((POOL_DIM, POOL_DIM)), _const_spec((1, POOL_DIM)),
                  _const_spec((D_MODEL, D_MODEL)), _const_spec((1, D_MODEL)), _const_spec((1, D_MODEL)),
                  _const_spec((D_FF, D_MODEL)), _const_spec((D_FF, D_MODEL)), _const_spec((D_FF, D_MODEL)),
                  _const_spec((1, D_MODEL))],
        out_specs=row(D_MODEL),
        out_shape=jax.ShapeDtypeStruct((batch, seq, D_MODEL), F32),
        scratch_shapes=[pool_buf, pool_buf, order_buf, order_buf, pltpu.VMEM((tm, D_MODEL), F32)],
        compiler_params=_params(2),
        name="mix_ffn",
    )(x, u, u, u, *attn, w_pool_bd, pool_scale, w_out, g_mix_post, g_pre, wg, wu, wd, g_post)


def _ffn_weights(w_gate, w_up, w_down):
    return w_gate.T.astype(BF16), w_up.T.astype(BF16), w_down.astype(BF16)


def _qkv_by_group(w_qkv):
    d_model = w_qkv.shape[0]
    n_groups = len(ATTN_GROUPS)
    return w_qkv.reshape(d_model, 3, n_groups, GROUP_DIM).transpose(0, 2, 1, 3).reshape(d_model, D_QKV)


def _block_diag(w_lin):
    g, c, e = w_lin.shape
    eye = jnp.eye(g, dtype=w_lin.dtype)
    return (eye[:, None, :, None] * w_lin[:, :, None, :]).reshape(g * c, g * e)


def kernel(x, g_ffn1_pre, w1_gate, w1_up, w1_down, g_ffn1_post, g_mix_pre, w_in, w_pool_lin, pool_scale,
           w_out, g_mix_post, g_ffn2_pre, w2_gate, w2_up, w2_down, g_ffn2_post):
    batch, seq, _ = x.shape
    depth = g_ffn1_pre.shape[0]
    tm = 512
    assert x.shape[-1] == D_MODEL and w1_gate.shape[1:] == (D_MODEL, D_FF) and w_in.shape[1:] == (D_MODEL, POOL_DIM + D_QKV)
    assert seq % tm == 0 and tm % (ATTN_GROUPS[-1][1] * 16) == 0
    h = x
    for l in range(depth):
        w_in_bf = w_in[l].astype(BF16)
        x1, u, *qkv = _ffn_proj(h, g_ffn1_pre[l][None], *_ffn_weights(w1_gate[l], w1_up[l], w1_down[l]),
                                g_ffn1_post[l][None], g_mix_pre[l][None], w_in_bf[:, :POOL_DIM],
                                _qkv_by_group(w_in_bf[:, POOL_DIM:]), tm=tm)
        attn = [_attention_group(qkv[gi], gi) for gi in range(len(ATTN_GROUPS))]
        h = _mix_ffn(x1, u, attn, _block_diag(w_pool_lin[l]).astype(BF16), pool_scale[l][None],
                     w_out[l].astype(BF16), g_mix_post[l][None], g_ffn2_pre[l][None],
                     *_ffn_weights(w2_gate[l], w2_up[l], w2_down[l]), g_ffn2_post[l][None], tm=tm)
    return h
```
